```python
import math
import jax, jax.numpy as jnp
from jax import lax
import numpy as np

D_MODEL = 4096
BATCH = 4
SEQ = 4096
DEPTH = 1

D_MIX = D_MODEL
D_SSM = D_MIX // 2
SSM_HEAD_DIM = 64
SSM_HEADS = D_SSM // SSM_HEAD_DIM
SSM_GROUPS = 8
SSM_STATE = 128
SSM_CONV = 4
SSM_CHUNK = 128
DT_MIN = 0.001
DT_MAX = 0.1
SSM_NORM_EPS = 1e-5
D_RWKV = D_MIX - D_SSM
RWKV_HEAD_DIM = 64
RWKV_HEADS = D_RWKV // RWKV_HEAD_DIM
DECAY_LORA = 96
AAA_LORA = 96
GATE_LORA = 256
RWKV_GN_EPS = 64e-5
L2_EPS = 1e-12
D_FF = 4 * D_MODEL
D_PLE = 256
LN_EPS = 1e-5
DEEPNORM_ALPHA = (2 * DEPTH) ** 0.25
DEEPNORM_BETA = (8 * DEPTH) ** -0.25

D_XBC = D_SSM + 2 * SSM_GROUPS * SSM_STATE
D_IN_SSM = D_SSM + D_XBC + SSM_HEADS
D_IN_RWKV = 3 * D_RWKV + DECAY_LORA + AAA_LORA + GATE_LORA
D_IN = D_IN_SSM + D_IN_RWKV

kernel_name = "hymba_ssd_rwkv7_deepnorm_block"


def _layer_norm(x, g, b):
    xf = x.astype(jnp.float32)
    mu = jnp.mean(xf, axis=-1, keepdims=True)
    var = jnp.mean(jnp.square(xf - mu), axis=-1, keepdims=True)
    y = (xf - mu) * lax.rsqrt(var + LN_EPS) * g.astype(jnp.float32) + b.astype(jnp.float32)
    return y.astype(x.dtype)


def _causal_depthwise_conv(u, w, b):
    C = u.shape[-1]
    out = lax.conv_general_dilated(
        u, w.astype(u.dtype)[:, None, :], window_strides=(1,),
        padding=[(w.shape[0] - 1, 0)], dimension_numbers=('NWC', 'WIO', 'NWC'),
        feature_group_count=C)
    return out + b.astype(u.dtype)


def _token_shift(u, mu):
    prev = jnp.pad(u, ((0, 0), (1, 0), (0, 0)))[:, :-1, :]
    return u + (prev - u) * mu.astype(u.dtype)


def _ssd_chunked(xs, dt, A, Bm, Cm):
    Bsz, L, H, P = xs.shape
    G, N = Bm.shape[-2:]
    E = H // G
    Q = SSM_CHUNK
    NC = L // Q
    x = (xs * dt[..., None]).reshape(Bsz, NC, Q, G, E, P)
    a = (dt * A).reshape(Bsz, NC, Q, G, E)
    Bc = Bm.reshape(Bsz, NC, Q, G, N)
    Cc = Cm.reshape(Bsz, NC, Q, G, N)
    a_cs = jnp.cumsum(a, axis=2)
    causal = jnp.tril(jnp.ones((Q, Q), dtype=bool))
    seg = a_cs[:, :, :, None] - a_cs[:, :, None, :]
    decay_ls = jnp.exp(jnp.where(causal[:, :, None, None], seg, -jnp.inf))
    scores = jnp.einsum('bclgn,bcsgn->bclsg', Cc, Bc)
    y_diag = jnp.einsum('bclsge,bcsgep->bclgep', scores[..., None] * decay_ls, x)
    decay_to_end = jnp.exp(a_cs[:, :, -1:] - a_cs)
    chunk_states = jnp.einsum('bcsgn,bcsgep->bcgepn', Bc, x * decay_to_end[..., None])
    chunk_decay = jnp.exp(a_cs[:, :, -1])

    def chunk_step(state, inp):
        s_c, d_c = inp
        return state * d_c[..., None, None] + s_c, state

    init = jnp.zeros((Bsz, G, E, P, N), jnp.float32)
    _, states_in = lax.scan(chunk_step, init,
                            (jnp.moveaxis(chunk_states, 1, 0), jnp.moveaxis(chunk_decay, 1, 0)))
    states_in = jnp.moveaxis(states_in, 0, 1)
    y_off = jnp.einsum('bclgn,bcgepn->bclgep', Cc, states_in) * jnp.exp(a_cs)[..., None]
    return (y_diag + y_off).reshape(Bsz, L, H, P)


def _mamba2_group(u, conv_w, conv_b, dt_bias, A_log, D_skip, norm_g):
    Bsz, L, _ = u.shape
    z, xbc, dt_raw = jnp.split(u, [D_SSM, D_SSM + D_XBC], axis=-1)
    xbc = jax.nn.silu(_causal_depthwise_conv(xbc, conv_w, conv_b)).astype(jnp.float32)
    xs, Bm, Cm = jnp.split(xbc, [D_SSM, D_SSM + SSM_GROUPS * SSM_STATE], axis=-1)
    xs = xs.reshape(Bsz, L, SSM_HEADS, SSM_HEAD_DIM)
    Bm = Bm.reshape(Bsz, L, SSM_GROUPS, SSM_STATE)
    Cm = Cm.reshape(Bsz, L, SSM_GROUPS, SSM_STATE)
    dt = jax.nn.softplus(dt_raw.astype(jnp.float32) + dt_bias.astype(jnp.float32))
    A = -jnp.exp(A_log.astype(jnp.float32))
    y = _ssd_chunked(xs, dt, A, Bm, Cm)
    y = y + D_skip.astype(jnp.float32)[:, None] * xs
    v = y.reshape(Bsz, L, D_SSM) * jax.nn.silu(z.astype(jnp.float32))
    v = v.reshape(Bsz, L, SSM_GROUPS, D_SSM // SSM_GROUPS)
    v = v * lax.rsqrt(jnp.mean(jnp.square(v), axis=-1, keepdims=True) + SSM_NORM_EPS)
    v = v.reshape(Bsz, L, D_SSM) * norm_g.astype(jnp.float32)
    return v.astype(u.dtype)


def _wkv7_scan(r, decay, k, v, kk, a):
    Bsz, L, H, N = r.shape

    def step(S, inp):
        r_t, w_t, k_t, v_t, kk_t, a_t = inp
        sa = jnp.einsum('bhij,bhj->bhi', S, kk_t)
        S = (S * w_t[:, :, None, :]
             - sa[..., None] * (kk_t * a_t)[:, :, None, :]
             + v_t[..., None] * k_t[:, :, None, :])
        return S, jnp.einsum('bhij,bhj->bhi', S, r_t)

    seq = tuple(jnp.moveaxis(t, 1, 0) for t in (r, decay, k, v, kk, a))
    S0 = jnp.zeros((Bsz, H, N, N), jnp.float32)
    _, y = lax.scan(step, S0, seq)
    return jnp.moveaxis(y, 0, 1)


def _rwkv7_group(u, mu, w0, w_decay_b, a0, w_aaa_b, w_gate_b, k_k, k_a, r_k, gn_g, gn_b):
    Bsz, L, _ = u.shape
    H, N = RWKV_HEADS, RWKV_HEAD_DIM
    f32 = jnp.float32
    s = _token_shift(u, mu).astype(f32)
    r, k, v, xw, xa, xg = jnp.split(
        s, [D_RWKV, 2 * D_RWKV, 3 * D_RWKV, 3 * D_RWKV + DECAY_LORA,
            3 * D_RWKV + DECAY_LORA + AAA_LORA], axis=-1)
    w_log = -jax.nn.softplus(-(w0.astype(f32) + jnp.tanh(xw) @ w_decay_b.astype(f32))) - 0.5
    decay = jnp.exp(-jnp.exp(w_log))
    a = jax.nn.sigmoid(a0.astype(f32) + xa @ w_aaa_b.astype(f32))
    g = jax.nn.sigmoid(xg) @ w_gate_b.astype(f32)
    kk = (k * k_k.astype(f32)).reshape(Bsz, L, H, N)
    kk = kk / jnp.maximum(jnp.linalg.norm(kk, axis=-1, keepdims=True), L2_EPS)
    k = k * (1.0 + (a - 1.0) * k_a.astype(f32))
    hd = lambda t: t.reshape(Bsz, L, H, N)
    r, k, v, decay, a = hd(r), hd(k), hd(v), hd(decay), hd(a)
    y = _wkv7_scan(r, decay, k, v, kk, a)
    m = jnp.mean(y, axis=-1, keepdims=True)
    var = jnp.mean(jnp.square(y - m), axis=-1, keepdims=True)
    y = (y - m) * lax.rsqrt(var + RWKV_GN_EPS)
    y = y * gn_g.astype(f32).reshape(H, N) + gn_b.astype(f32).reshape(H, N)
    y = y + jnp.sum(r * k * r_k.astype(f32), axis=-1, keepdims=True) * v
    y = y.reshape(Bsz, L, D_RWKV) * g
    return y.astype(u.dtype)


def setup_inputs(seed: int = 0) -> dict:
    key = jax.random.key(seed)
    ks = jax.random.split(key, 32)
    f32 = jnp.float32
    nrm = lambda k, shape, scale: jax.random.normal(k, shape, f32) * scale
    beta = DEEPNORM_BETA

    x = nrm(ks[0], (BATCH, SEQ, D_MODEL), 1.0)
    p = nrm(ks[1], (DEPTH, BATCH, SEQ, D_PLE), 1.0)

    col_scale = jnp.ones((D_IN,), f32)
    col_scale = col_scale.at[D_SSM:2 * D_SSM].set(beta)
    col_scale = col_scale.at[D_IN_SSM + 2 * D_RWKV:D_IN_SSM + 3 * D_RWKV].set(beta)
    w_in = nrm(ks[2], (DEPTH, D_MODEL, D_IN), D_MODEL ** -0.5) * col_scale

    conv_w = nrm(ks[3], (DEPTH, SSM_CONV, D_XBC), SSM_CONV ** -0.5)
    conv_b = nrm(ks[4], (DEPTH, D_XBC), 0.01)
    u = jax.random.uniform(ks[5], (DEPTH, SSM_HEADS), f32)
    dt0 = jnp.exp(u * (math.log(DT_MAX) - math.log(DT_MIN)) + math.log(DT_MIN))
    dt_bias = dt0 + jnp.log(-jnp.expm1(-dt0))
    A_log = jnp.log(jax.random.uniform(ks[6], (DEPTH, SSM_HEADS), f32, 1.0, 16.0))
    D_skip = 1.0 + nrm(ks[7], (DEPTH, SSM_HEADS), 0.1)
    ssm_norm_g = 1.0 + nrm(ks[8], (DEPTH, D_SSM), 0.05)

    rwkv_mu = jax.random.uniform(ks[9], (DEPTH, D_IN_RWKV), f32)
    w0 = jax.random.uniform(ks[10], (DEPTH, D_RWKV), f32, -6.0, 0.0)
    w_decay_b = nrm(ks[11], (DEPTH, DECAY_LORA, D_RWKV), 0.5 * DECAY_LORA ** -0.5)
    a0 = nrm(ks[12], (DEPTH, D_RWKV), 0.1)
    w_aaa_b = nrm(ks[13], (DEPTH, AAA_LORA, D_RWKV), AAA_LORA ** -0.5)
    w_gate_b = nrm(ks[14], (DEPTH, GATE_LORA, D_RWKV), GATE_LORA ** -0.5)
    k_k = 0.85 + nrm(ks[15], (DEPTH, D_RWKV), 0.05)
    k_a = 1.0 + nrm(ks[16], (DEPTH, D_RWKV), 0.05)
    r_k = -0.04 + nrm(ks[17], (DEPTH, RWKV_HEADS, RWKV_HEAD_DIM), 0.02)
    gn_g = 1.0 + nrm(ks[18], (DEPTH, D_RWKV), 0.05)
    gn_b = nrm(ks[19], (DEPTH, D_RWKV), 0.01)

    w_out = nrm(ks[20], (DEPTH, D_MIX, D_MODEL), D_MIX ** -0.5 * beta)
    ln1_g = 1.0 + nrm(ks[21], (DEPTH, D_MODEL), 0.05)
    ln1_b = nrm(ks[22], (DEPTH, D_MODEL), 0.01)
    w_up = nrm(ks[23], (DEPTH, D_MODEL, D_FF), D_MODEL ** -0.5 * beta)
    w_down = nrm(ks[24], (DEPTH, D_FF, D_MODEL), D_FF ** -0.5 * beta)
    ln2_g = 1.0 + nrm(ks[25], (DEPTH, D_MODEL), 0.05)
    ln2_b = nrm(ks[26], (DEPTH, D_MODEL), 0.01)
    w_ple = nrm(ks[27], (DEPTH, D_PLE, D_MODEL), D_PLE ** -0.5 * beta)
    w_ple_gate = nrm(ks[28], (DEPTH, D_MODEL, D_MODEL), D_MODEL ** -0.5)
    ln3_g = 1.0 + nrm(ks[29], (DEPTH, D_MODEL), 0.05)
    ln3_b = nrm(ks[30], (DEPTH, D_MODEL), 0.01)
    return {
        "x": x, "p": p, "w_in": w_in,
        "conv_w": conv_w, "conv_b": conv_b, "dt_bias": dt_bias, "A_log": A_log,
        "D_skip": D_skip, "ssm_norm_g": ssm_norm_g,
        "rwkv_mu": rwkv_mu, "w0": w0, "w_decay_b": w_decay_b, "a0": a0,
        "w_aaa_b": w_aaa_b, "w_gate_b": w_gate_b, "k_k": k_k, "k_a": k_a,
        "r_k": r_k, "gn_g": gn_g, "gn_b": gn_b,
        "w_out": w_out, "ln1_g": ln1_g, "ln1_b": ln1_b,
        "w_up": w_up, "w_down": w_down, "ln2_g": ln2_g, "ln2_b": ln2_b,
        "w_ple": w_ple, "w_ple_gate": w_ple_gate, "ln3_g": ln3_g, "ln3_b": ln3_b,
    }


def reference(x, p, w_in, conv_w, conv_b, dt_bias, A_log, D_skip, ssm_norm_g,
              rwkv_mu, w0, w_decay_b, a0, w_aaa_b, w_gate_b, k_k, k_a, r_k, gn_g, gn_b,
              w_out, ln1_g, ln1_b, w_up, w_down, ln2_g, ln2_b,
              w_ple, w_ple_gate, ln3_g, ln3_b):
    h = x
    for i in range(DEPTH):
        u = h @ w_in[i]
        y_ssm = _mamba2_group(u[..., :D_IN_SSM], conv_w[i], conv_b[i], dt_bias[i],
                              A_log[i], D_skip[i], ssm_norm_g[i])
        y_rwkv = _rwkv7_group(u[..., D_IN_SSM:], rwkv_mu[i], w0[i], w_decay_b[i], a0[i],
                              w_aaa_b[i], w_gate_b[i], k_k[i], k_a[i], r_k[i], gn_g[i], gn_b[i])
        mix = jnp.concatenate([y_ssm, y_rwkv], axis=-1) @ w_out[i]
        h = _layer_norm(DEEPNORM_ALPHA * h + mix, ln1_g[i], ln1_b[i])
        ff = jnp.square(jax.nn.relu(h @ w_up[i])) @ w_down[i]
        h = _layer_norm(DEEPNORM_ALPHA * h + ff, ln2_g[i], ln2_b[i])
        e = (p[i] @ w_ple[i]) * jax.nn.sigmoid(h @ w_ple_gate[i])
        h = _layer_norm(DEEPNORM_ALPHA * h + e, ln3_g[i], ln3_b[i])
    return h
```

```python
import functools

import jax
import jax.numpy as jnp
from jax import lax
from jax.experimental import pallas as pl
from jax.experimental.pallas import tpu as pltpu

F32 = jnp.float32
BF16 = jnp.bfloat16

HEAD = 64
PAIR = 2 * HEAD
NSTATE = 128
SSD_CHUNK = 128
RWKV_CHUNK = 64
SMALL_W = 512
DT_MIN_EPS = 1e-5
RWKV_GN_EPS = 64e-5
L2_EPS = 1e-12
LN_EPS = 1e-5
VMEM_LIMIT = 56 * 1024 * 1024

_DIMS = {
    "nn": (((1,), (0,)), ((), ())),
    "nt": (((1,), (1,)), ((), ())),
    "tn": (((0,), (0,)), ((), ())),
}


def _split(x, n):
    if x.dtype == BF16:
        return [x]
    parts = []
    rem = x
    for i in range(n):
        p = rem.astype(BF16)
        parts.append(p)
        if i + 1 < n:
            rem = rem - p.astype(F32)
    return parts


def _mm(a, b, dims="nn", pa=1, pb=1):
    aps = _split(a, pa)
    bps = _split(b, pb)
    lim = max(len(aps), len(bps))
    out = None
    for i in reversed(range(len(aps))):
        for j in reversed(range(len(bps))):
            if i + j < lim:
                t = lax.dot_general(aps[i], bps[j], _DIMS[dims], preferred_element_type=F32)
                out = t if out is None else out + t
    return out


def _softplus(x):
    return jnp.maximum(x, 0.0) + jnp.log(1.0 + jnp.exp(-jnp.abs(x)))


def _shift_rows(cur, prev8, j):
    rc = pltpu.roll(cur, j, 0)
    rp = pltpu.roll(prev8, j, 0)
    row = lax.broadcasted_iota(jnp.int32, rp.shape, 0)
    first = jnp.where(row < j, rp, rc[0:8, :])
    return jnp.concatenate([first, rc[8:, :]], axis=0)


def _block_diag(x, m0):
    return jnp.concatenate([jnp.where(m0, x, 0.0), jnp.where(m0, 0.0, x)], axis=0)


def _inproj_kernel(x_ref, w_ref, o_ref, xb_ref):
    @pl.when(pl.program_id(1) == 0)
    def _():
        xb_ref[...] = x_ref[...].astype(BF16)

    o_ref[...] = jnp.dot(xb_ref[...], w_ref[...], preferred_element_type=F32)


def _inproj(x2d, w_bf16, tm, tn):
    T, D = x2d.shape
    N = w_bf16.shape[1]
    return pl.pallas_call(
        _inproj_kernel,
        out_shape=jax.ShapeDtypeStruct((T, N), F32),
        grid=(T // tm, N // tn),
        in_specs=[
            pl.BlockSpec((tm, D), lambda i, j: (i, 0)),
            pl.BlockSpec((D, tn), lambda i, j: (0, j)),
        ],
        out_specs=pl.BlockSpec((tm, tn), lambda i, j: (i, j)),
        scratch_shapes=[pltpu.VMEM((tm, D), BF16)],
        compiler_params=pltpu.CompilerParams(
            dimension_semantics=("arbitrary", "arbitrary"), vmem_limit_bytes=VMEM_LIMIT),
        name="inproj",
    )(x2d, w_bf16)


def _conv_silu(cur_ref, prev_ref, w_ref, b_ref):
    cur = cur_ref[...]
    prev8 = prev_ref[...]
    q = cur.shape[0]
    acc = b_ref[...] + w_ref[3:4, :] * cur
    for j in (1, 2, 3):
        acc = acc + w_ref[3 - j:4 - j, :] * _shift_rows(cur, prev8, j)
    prev_ref[...] = cur[q - 8:, :]
    return acc * jax.nn.sigmoid(acc)


def _ssd_kernel(z_ref, xs_ref, bc_ref, sm_ref, cwx_ref, cbx_ref, cwbc_ref, cbbc_ref,
                dtb_ref, alog_ref, dsk_ref, ng_ref, e_ref, o_ref,
                st_ref, px_ref, pbc_ref, xc_ref, bcc_ref, ex_ref, *, G, H, dt_off, P):
    Q = xs_ref.shape[0]
    N = NSTATE
    GW = 4 * HEAD

    @pl.when(pl.program_id(1) == 0)
    def _():
        st_ref[...] = jnp.zeros_like(st_ref)
        px_ref[...] = jnp.zeros_like(px_ref)
        pbc_ref[...] = jnp.zeros_like(pbc_ref)

    xc_ref[...] = _conv_silu(xs_ref, px_ref, cwx_ref, cbx_ref)
    bcc_ref[...] = _conv_silu(bc_ref, pbc_ref, cwbc_ref, cbbc_ref)

    tile = dt_off // 128
    dt_lane = dt_off % 128
    lane = lax.broadcasted_iota(jnp.int32, (1, 128), 1)
    hmask = (lane >= dt_lane) & (lane < dt_lane + H)
    raw = sm_ref[:, tile * 128:(tile + 1) * 128]
    dt = jnp.where(hmask, _softplus(raw + dtb_ref[...]), 0.0)
    a_neg = jnp.where(hmask, -jnp.exp(alog_ref[...]), 0.0)
    a = dt * a_neg
    row = lax.broadcasted_iota(jnp.int32, (Q, Q), 0)
    col = lax.broadcasted_iota(jnp.int32, (Q, Q), 1)
    causal = col <= row
    tril = jnp.where(causal, 1.0, 0.0).astype(BF16)
    a_cs = _mm(tril, a, "nn", 1, 3)
    a_last = a_cs[Q - 1:Q, :]
    a_cs_t = a_cs.T
    stack = jnp.concatenate(
        [dt, jnp.exp(a_cs), jnp.exp(a_last - a_cs), jnp.broadcast_to(jnp.exp(a_last), (8, 128))], axis=0)
    ex_ref[...] = _mm(stack, e_ref[...], "nn", 3, 1)

    lane_p = lax.broadcasted_iota(jnp.int32, (1, PAIR), 1)
    m0 = lane_p < HEAD
    for g in range(G):
        gs = slice(g * GW, (g + 1) * GW)
        bg = bcc_ref[:, g * N:(g + 1) * N]
        cg = bcc_ref[:, G * N + g * N:G * N + (g + 1) * N]
        scores = _mm(cg, bg, "nt", P, P)
        xsg = xc_ref[:, gs]
        xdt = xsg * ex_ref[0:Q, gs]
        st = st_ref[g]
        y = _mm(cg, st, "nn", P, P) * ex_ref[Q:2 * Q, gs]
        ypairs = []
        for pr in range(2):
            xp = xdt[:, pr * PAIR:(pr + 1) * PAIR]
            acc = None
            for half in range(2):
                hl = dt_lane + g * 4 + pr * 2 + half
                seg = a_cs[:, hl:hl + 1] - a_cs_t[hl:hl + 1, :]
                lm = jnp.exp(jnp.where(causal, seg, -jnp.inf))
                xm = jnp.where(m0, xp, 0.0) if half == 0 else jnp.where(m0, 0.0, xp)
                t = _mm(scores * lm, xm, "nn", P, P)
                acc = t if acc is None else acc + t
            ypairs.append(acc)
        y = y + jnp.concatenate(ypairs, axis=1)
        st_ref[g] = st * ex_ref[3 * Q:3 * Q + 1, gs] + _mm(bg.T, xdt * ex_ref[2 * Q:3 * Q, gs], "nn", P, P)
        y = y + dsk_ref[:, gs] * xsg
        zg = z_ref[:, gs]
        v = y * (zg * jax.nn.sigmoid(zg))
        ms = jnp.mean(v * v, axis=-1, keepdims=True)
        o_ref[:, gs] = (v * lax.rsqrt(ms + DT_MIN_EPS) * ng_ref[:, gs]).astype(BF16)


def _ssd(u, B, L, D_SSM, G, H, dt_off, small_blk, cwx, cbx, cwbc, cbbc, dtb, alog, dsk, ng, emat, P):
    Q = SSD_CHUNK
    NC = L // Q
    T = B * L
    full = lambda shape: pl.BlockSpec(shape, lambda b, c: (0,) * len(shape))
    rowblk = lambda w, j: pl.BlockSpec((Q, w), lambda b, c: (b * NC + c, j))
    return pl.pallas_call(
        functools.partial(_ssd_kernel, G=G, H=H, dt_off=dt_off, P=P),
        out_shape=jax.ShapeDtypeStruct((T, D_SSM), BF16),
        grid=(B, NC),
        in_specs=[
            rowblk(D_SSM, 0), rowblk(D_SSM, 1), rowblk(D_SSM, 2), rowblk(SMALL_W, small_blk),
            full(cwx.shape), full(cbx.shape), full(cwbc.shape), full(cbbc.shape),
            full(dtb.shape), full(alog.shape), full(dsk.shape), full(ng.shape), full(emat.shape),
        ],
        out_specs=pl.BlockSpec((Q, D_SSM), lambda b, c: (b * NC + c, 0)),
        scratch_shapes=[
            pltpu.VMEM((G, NSTATE, 4 * HEAD), F32),
            pltpu.VMEM((8, D_SSM), F32),
            pltpu.VMEM((8, D_SSM), F32),
            pltpu.VMEM((Q, D_SSM), F32),
            pltpu.VMEM((Q, D_SSM), F32),
            pltpu.VMEM((3 * Q + 8, D_SSM), F32),
        ],
        compiler_params=pltpu.CompilerParams(
            dimension_semantics=("arbitrary", "arbitrary"), vmem_limit_bytes=VMEM_LIMIT),
        name="ssd",
    )(u, u, u, u, cwx, cbx, cwbc, cbbc, dtb, alog, dsk, ng, emat)


def _rwkv_kernel(r_ref, k_ref, v_ref, sm_ref, mur_ref, muk_ref, muv_ref, mus_ref,
                 w0_ref, a0_ref, kk_ref, ka_ref, rk_ref, gng_ref, gnb_ref,
                 wd_ref, wa_ref, wg_ref, o_ref,
                 st_ref, pr_ref, pk_ref, pv_ref, ps_ref, th_ref, xa_ref, sg_ref, *, NP, UNR, P, PL):
    C = r_ref.shape[0]

    @pl.when(pl.program_id(1) == 0)
    def _():
        st_ref[...] = jnp.zeros_like(st_ref)
        pr_ref[...] = jnp.zeros_like(pr_ref)
        pk_ref[...] = jnp.zeros_like(pk_ref)
        pv_ref[...] = jnp.zeros_like(pv_ref)
        ps_ref[...] = jnp.zeros_like(ps_ref)

    def shift_lerp(cur, prev8, mu):
        prev = _shift_rows(cur, prev8, 1)
        return cur + (prev - cur) * mu

    sm = sm_ref[...]
    ss = shift_lerp(sm, ps_ref[...], mus_ref[...])
    ps_ref[...] = sm[C - 8:, :]
    th_ref[...] = jnp.tanh(ss[:, 0:128])
    xa_ref[...] = ss[:, 0:256]
    sg_ref[...] = jax.nn.sigmoid(ss[:, 128:512])

    lane = lax.broadcasted_iota(jnp.int32, (C, PAIR), 1)
    rowi = lax.broadcasted_iota(jnp.int32, (C, PAIR), 0)
    m0 = lane < HEAD
    s_idx = jnp.where(m0, lane, lane - HEAD)
    strict = s_idx < rowi
    incl = s_idx <= rowi
    eye_p = jnp.where(s_idx == rowi, 1.0, 0.0)
    r2 = lax.broadcasted_iota(jnp.int32, (PAIR, PAIR), 0)
    c2 = lax.broadcasted_iota(jnp.int32, (PAIR, PAIR), 1)
    ones_bd = jnp.where((r2 < HEAD) == (c2 < HEAD), 1.0, 0.0).astype(BF16)
    r3 = lax.broadcasted_iota(jnp.int32, (C, C), 0)
    c3 = lax.broadcasted_iota(jnp.int32, (C, C), 1)
    tril = jnp.where(c3 <= r3, 1.0, 0.0).astype(BF16)
    n_dbl = C.bit_length() - 2

    def one_pair(ds):
        r_raw = r_ref[:, ds]
        k_raw = k_ref[:, ds]
        v_raw = v_ref[:, ds]
        s_r = shift_lerp(r_raw, pr_ref[:, ds], mur_ref[:, ds])
        s_k = shift_lerp(k_raw, pk_ref[:, ds], muk_ref[:, ds])
        v = shift_lerp(v_raw, pv_ref[:, ds], muv_ref[:, ds])
        pr_ref[:, ds] = r_raw[C - 8:, :]
        pk_ref[:, ds] = k_raw[C - 8:, :]
        pv_ref[:, ds] = v_raw[C - 8:, :]

        lw = _mm(th_ref[...], wd_ref[:, ds], "nn", PL, PL)
        ld = -jnp.exp(-_softplus(-(w0_ref[:, ds] + lw)) - 0.5)
        a_sig = jax.nn.sigmoid(a0_ref[:, ds] + _mm(xa_ref[...], wa_ref[:, ds], "nn", PL, PL))
        gate = _mm(sg_ref[...], wg_ref[:, ds], "nn", PL, PL)
        kkr = s_k * kk_ref[:, ds]
        n2 = _mm(kkr * kkr, ones_bd, "nn", 3, 1)
        kk = kkr / jnp.maximum(jnp.sqrt(n2), L2_EPS)
        k2 = s_k * (1.0 + (a_sig - 1.0) * ka_ref[:, ds])
        bvec = kk * a_sig
        cs = _mm(tril, ld, "nn", 1, 3)
        cs_last = cs[C - 1:C, :]
        e_in = jnp.exp(-cs)
        e_end = jnp.exp(cs_last - cs)
        at = -kk * jnp.exp(cs - ld)
        rt = s_r * jnp.exp(cs)
        kt = k2 * e_in
        bt = bvec * e_in
        s0 = st_ref[:, ds]

        lhs = jnp.concatenate([at, rt], axis=0)
        ak = _mm(lhs, _block_diag(kt, m0), "nt", P, P)
        ab = _mm(lhs, _block_diag(bt, m0), "nt", P, P)
        a_ak = jnp.where(strict, ak[0:C], 0.0)
        a_rk = jnp.where(incl, ak[C:2 * C], 0.0)
        a_ab = jnp.where(strict, ab[0:C], 0.0)
        a_rb = jnp.where(incl, ab[C:2 * C], 0.0)
        m = a_ab
        tinv = eye_p + m
        for _ in range(n_dbl):
            m = _mm(m, _block_diag(m, m0), "nn", P, P)
            tinv = tinv + _mm(tinv, _block_diag(m, m0), "nn", P, P)
        ss0 = _mm(lhs, _block_diag(s0, m0), "nt", P, P)
        bdv = _block_diag(v, m0)
        u = _mm(tinv, _block_diag(ss0[0:C] + _mm(a_ak, bdv, "nn", P, P), m0), "nn", P, P)
        y = ss0[C:2 * C] + _mm(a_rk, bdv, "nn", P, P) + _mm(a_rb, _block_diag(u, m0), "nn", P, P)
        vu = jnp.concatenate([v, u], axis=0)
        kb = jnp.concatenate([k2 * e_end, bvec * e_end], axis=0)
        dl = _mm(vu.T, kb, "nn", P, P)
        st_ref[:, ds] = s0 * jnp.exp(cs_last) + jnp.where(m0, dl[0:HEAD], dl[HEAD:PAIR])

        inv_n = 1.0 / HEAD
        mean = _mm(y, ones_bd, "nn", 3, 1) * inv_n
        d = y - mean
        var = _mm(d * d, ones_bd, "nn", 3, 1) * inv_n
        yn = d * lax.rsqrt(var + RWKV_GN_EPS) * gng_ref[:, ds] + gnb_ref[:, ds]
        bonus = _mm(s_r * k2 * rk_ref[:, ds], ones_bd, "nn", 3, 1)
        o_ref[:, ds] = ((yn + bonus * v) * gate).astype(BF16)

    def body(i, carry):
        for k in range(UNR):
            one_pair(pl.ds(pl.multiple_of((i * UNR + k) * PAIR, PAIR), PAIR))
        return carry

    lax.fori_loop(0, NP // UNR, body, 0)


def _rwkv(u, B, L, D_R, rkv_blk, small_blk, params, P, PL, UNR):
    C = RWKV_CHUNK
    NC = L // C
    T = B * L
    NP = D_R // PAIR
    full = lambda a: pl.BlockSpec(a.shape, lambda b, c: (0,) * a.ndim)
    rowblk = lambda w, j: pl.BlockSpec((C, w), lambda b, c: (b * NC + c, j))
    return pl.pallas_call(
        functools.partial(_rwkv_kernel, NP=NP, UNR=UNR, P=P, PL=PL),
        out_shape=jax.ShapeDtypeStruct((T, D_R), BF16),
        grid=(B, NC),
        in_specs=[rowblk(D_R, rkv_blk), rowblk(D_R, rkv_blk + 1), rowblk(D_R, rkv_blk + 2),
                  rowblk(SMALL_W, small_blk)] + [full(a) for a in params],
        out_specs=pl.BlockSpec((C, D_R), lambda b, c: (b * NC + c, 0)),
        scratch_shapes=[
            pltpu.VMEM((HEAD, D_R), F32),
            pltpu.VMEM((8, D_R), F32), pltpu.VMEM((8, D_R), F32), pltpu.VMEM((8, D_R), F32),
            pltpu.VMEM((8, SMALL_W), F32),
            pltpu.VMEM((C, 128), F32), pltpu.VMEM((C, 256), F32), pltpu.VMEM((C, 384), F32),
        ],
        compiler_params=pltpu.CompilerParams(
            dimension_semantics=("arbitrary", "arbitrary"), vmem_limit_bytes=VMEM_LIMIT),
        name="rwkv",
    )(u, u, u, u, *params)


LN_ROWS = 32


def _layer_norm_rows(o_ref, g_ref, b_ref):
    tm = o_ref.shape[0]

    def body(i, carry):
        rs = pl.ds(pl.multiple_of(i * LN_ROWS, LN_ROWS), LN_ROWS)
        h = o_ref[rs, :]
        mu = jnp.mean(h, axis=-1, keepdims=True)
        d = h - mu
        var = jnp.mean(d * d, axis=-1, keepdims=True)
        o_ref[rs, :] = d * lax.rsqrt(var + LN_EPS) * g_ref[...] + b_ref[...]
        return carry

    lax.fori_loop(0, tm // LN_ROWS, body, 0)


def _outproj_ln_kernel(y1_ref, y2_ref, w1_ref, w2_ref, res_ref, g_ref, b_ref, o_ref, *, alpha, tn, nj):
    j = pl.program_id(1)
    acc = jnp.dot(y1_ref[...], w1_ref[...], preferred_element_type=F32)
    acc = acc + jnp.dot(y2_ref[...], w2_ref[...], preferred_element_type=F32)
    cs = pl.ds(pl.multiple_of(j * tn, tn), tn)
    o_ref[:, cs] = alpha * res_ref[...] + acc

    @pl.when(j == nj - 1)
    def _():
        _layer_norm_rows(o_ref, g_ref, b_ref)


def _outproj_ln(y1, y2, w1, w2, res, g, b, alpha, tm, tn):
    T, K1 = y1.shape
    K2 = y2.shape[1]
    D = w1.shape[1]
    nj = D // tn
    return pl.pallas_call(
        functools.partial(_outproj_ln_kernel, alpha=alpha, tn=tn, nj=nj),
        out_shape=jax.ShapeDtypeStruct((T, D), F32),
        grid=(T // tm, nj),
        in_specs=[
            pl.BlockSpec((tm, K1), lambda i, j: (i, 0)),
            pl.BlockSpec((tm, K2), lambda i, j: (i, 0)),
            pl.BlockSpec((K1, tn), lambda i, j: (0, j)),
            pl.BlockSpec((K2, tn), lambda i, j: (0, j)),
            pl.BlockSpec((tm, tn), lambda i, j: (i, j)),
            pl.BlockSpec((1, D), lambda i, j: (0, 0)),
            pl.BlockSpec((1, D), lambda i, j: (0, 0)),
        ],
        out_specs=pl.BlockSpec((tm, D), lambda i, j: (i, 0)),
        compiler_params=pltpu.CompilerParams(
            dimension_semantics=("arbitrary", "arbitrary"), vmem_limit_bytes=VMEM_LIMIT),
        name="outproj_ln",
    )(y1, y2, w1, w2, res, g, b)


def _ffn_ln_kernel(h_ref, wu_ref, wd_ref, g_ref, b_ref, o_ref, hb_ref, *, alpha, nf):
    f = pl.program_id(1)

    @pl.when(f == 0)
    def _():
        h = h_ref[...]
        hb_ref[...] = h.astype(BF16)
        o_ref[...] = alpha * h

    a = jnp.dot(hb_ref[...], wu_ref[...], preferred_element_type=F32)
    a = jnp.maximum(a, 0.0)
    a = (a * a).astype(BF16)
    o_ref[...] += jnp.dot(a, wd_ref[...], preferred_element_type=F32)

    @pl.when(f == nf - 1)
    def _():
        _layer_norm_rows(o_ref, g_ref, b_ref)


def _ffn_ln(h, wu, wd, g, b, alpha, tm, tf):
    T, D = h.shape
    F = wu.shape[1]
    nf = F // tf
    return pl.pallas_call(
        functools.partial(_ffn_ln_kernel, alpha=alpha, nf=nf),
        out_shape=jax.ShapeDtypeStruct((T, D), F32),
        grid=(T // tm, nf),
        in_specs=[
            pl.BlockSpec((tm, D), lambda i, f: (i, 0), pipeline_mode=pl.Buffered(1)),
            pl.BlockSpec((D, tf), lambda i, f: (0, f)),
            pl.BlockSpec((tf, D), lambda i, f: (f, 0)),
            pl.BlockSpec((1, D), lambda i, f: (0, 0)),
            pl.BlockSpec((1, D), lambda i, f: (0, 0)),
        ],
        out_specs=pl.BlockSpec((tm, D), lambda i, f: (i, 0)),
        scratch_shapes=[pltpu.VMEM((tm, D), BF16)],
        compiler_params=pltpu.CompilerParams(
            dimension_semantics=("arbitrary", "arbitrary"), vmem_limit_bytes=VMEM_LIMIT),
        name="ffn_ln",
    )(h, wu, wd, g, b)


def _ple_ln_kernel(h_ref, p_ref, wg_ref, wp_ref, g_ref, b_ref, o_ref, hb_ref, *, alpha, tn, nj):
    j = pl.program_id(1)

    @pl.when(j == 0)
    def _():
        hb_ref[...] = h_ref[...].astype(BF16)

    gate = jax.nn.sigmoid(jnp.dot(hb_ref[...], wg_ref[...], preferred_element_type=F32))
    emb = jnp.dot(p_ref[...].astype(BF16), wp_ref[...], preferred_element_type=F32)
    cs = pl.ds(pl.multiple_of(j * tn, tn), tn)
    o_ref[:, cs] = alpha * h_ref[:, cs] + emb * gate

    @pl.when(j == nj - 1)
    def _():
        _layer_norm_rows(o_ref, g_ref, b_ref)


def _ple_ln(h, p2d, wg, wp, g, b, alpha, tm, tn):
    T, D = h.shape
    DP = p2d.shape[1]
    nj = D // tn
    return pl.pallas_call(
        functools.partial(_ple_ln_kernel, alpha=alpha, tn=tn, nj=nj),
        out_shape=jax.ShapeDtypeStruct((T, D), F32),
        grid=(T // tm, nj),
        in_specs=[
            pl.BlockSpec((tm, D), lambda i, j: (i, 0)),
            pl.BlockSpec((tm, DP), lambda i, j: (i, 0)),
            pl.BlockSpec((D, tn), lambda i, j: (0, j)),
            pl.BlockSpec((DP, tn), lambda i, j: (0, j)),
            pl.BlockSpec((1, D), lambda i, j: (0, 0)),
            pl.BlockSpec((1, D), lambda i, j: (0, 0)),
        ],
        out_specs=pl.BlockSpec((tm, D), lambda i, j: (i, 0)),
        scratch_shapes=[pltpu.VMEM((tm, D), BF16)],
        compiler_params=pltpu.CompilerParams(
            dimension_semantics=("arbitrary", "arbitrary"), vmem_limit_bytes=VMEM_LIMIT),
        name="ple_ln",
    )(h, p2d, wg, wp, g, b)


def _pad_rows(w, rows, offset=0):
    out = jnp.zeros((rows, w.shape[1]), w.dtype)
    return out.at[offset:offset + w.shape[0]].set(w)


def _layer(h, p_i, w_in, conv_w, conv_b, dt_bias, A_log, D_skip, ssm_norm_g,
           rwkv_mu, w0, w_decay_b, a0, w_aaa_b, w_gate_b, k_k, k_a, r_k, gn_g, gn_b,
           w_out, ln1_g, ln1_b, w_up, w_down, ln2_g, ln2_b, w_ple, w_ple_gate, ln3_g, ln3_b, alpha):
    B, L, D = h.shape
    T = B * L
    D_SSM = ssm_norm_g.shape[0]
    H = dt_bias.shape[0]
    D_XBC = conv_w.shape[1]
    G = (D_XBC - D_SSM) // (2 * NSTATE)
    D_R = w0.shape[0]
    dl, al, gl = w_decay_b.shape[0], w_aaa_b.shape[0], w_gate_b.shape[0]
    dt_off = dl + al + gl
    assert D_XBC == 2 * D_SSM and D_R == D_SSM and D_SSM == H * HEAD and H == 4 * G
    assert dl <= 128 and dl + al <= 256 and 128 <= dl + al and dt_off + H <= SMALL_W
    assert dt_off // 128 == (dt_off + H - 1) // 128
    assert L % SSD_CHUNK == 0 and L % RWKV_CHUNK == 0 and D_R % PAIR == 0

    o_xbc = D_SSM
    o_dt = D_SSM + D_XBC
    o_r = o_dt + H
    o_lo = o_r + 3 * D_R
    pad = SMALL_W - (dt_off + H)
    w_cat = jnp.concatenate(
        [w_in[:, :o_dt], w_in[:, o_r:o_lo], w_in[:, o_lo:], w_in[:, o_dt:o_r], jnp.zeros((D, pad), w_in.dtype)],
        axis=1).astype(BF16)
    small_blk = (D_SSM + D_XBC + 3 * D_R) // SMALL_W
    rkv_blk = (D_SSM + D_XBC) // D_R

    tm = min(512, T)
    u = _inproj(h.reshape(T, D), w_cat, tm, 512)

    row = lambda a: a.reshape(1, -1).astype(F32)
    lane_tile = lambda a: jnp.zeros((1, 128), F32).at[0, dt_off % 128:dt_off % 128 + H].set(a)
    hl = jnp.arange(128)[:, None] - dt_off % 128
    emat = (hl == (jnp.arange(D_SSM)[None, :] // HEAD)).astype(BF16)
    y_ssm = _ssd(u, B, L, D_SSM, G, H, dt_off, small_blk,
                 conv_w[:, :D_SSM], row(conv_b[:D_SSM]), conv_w[:, D_SSM:], row(conv_b[D_SSM:]),
                 lane_tile(dt_bias), lane_tile(A_log), row(jnp.repeat(D_skip, HEAD)), row(ssm_norm_g), emat, P=2)

    mu = rwkv_mu
    mus = jnp.concatenate([mu[3 * D_R:], jnp.zeros((SMALL_W - dt_off,), F32)])
    params = [row(mu[:D_R]), row(mu[D_R:2 * D_R]), row(mu[2 * D_R:3 * D_R]), row(mus),
              row(w0), row(a0), row(k_k), row(k_a), row(r_k), row(gn_g), row(gn_b),
              _pad_rows(w_decay_b, 128), _pad_rows(w_aaa_b, 256, dl), _pad_rows(w_gate_b, 384, dl + al - 128)]
    y_rwkv = _rwkv(u, B, L, D_R, rkv_blk, small_blk, params, P=2, PL=2, UNR=2)

    wo = w_out.astype(BF16)
    h1 = _outproj_ln(y_ssm, y_rwkv, wo[:D_SSM], wo[D_SSM:], h.reshape(T, D), row(ln1_g), row(ln1_b), alpha, tm, 512)
    h2 = _ffn_ln(h1, w_up.astype(BF16), w_down.astype(BF16), row(ln2_g), row(ln2_b), alpha, tm, 512)
    h3 = _ple_ln(h2, p_i.reshape(T, -1), w_ple_gate.astype(BF16), w_ple.astype(BF16),
                 row(ln3_g), row(ln3_b), alpha, tm, 512)
    return h3.reshape(B, L, D)


def kernel(x, p, w_in, conv_w, conv_b, dt_bias, A_log, D_skip, ssm_norm_g, rwkv_mu, w0, w_decay_b, a0, w_aaa_b, w_gate_b, k_k, k_a, r_k, gn_g, gn_b, w_out, ln1_g, ln1_b, w_up, w_down, ln2_g, ln2_b, w_ple, w_ple_gate, ln3_g, ln3_b):
    depth = w_in.shape[0]
    alpha = float((2 * depth) ** 0.25)
    h = x
    for i in range(depth):
        h = _layer(h, p[i], w_in[i], conv_w[i], conv_b[i], dt_bias[i], A_log[i], D_skip[i], ssm_norm_g[i],
                   rwkv_mu[i], w0[i], w_decay_b[i], a0[i], w_aaa_b[i], w_gate_b[i], k_k[i], k_a[i],
                   r_k[i].reshape(-1), gn_g[i], gn_b[i], w_out[i], ln1_g[i], ln1_b[i], w_up[i], w_down[i],
                   ln2_g[i], ln2_b[i], w_ple[i], w_ple_gate[i], ln3_g[i], ln3_b[i], alpha)
    return h
```

```python
import functools

import jax
import jax.numpy as jnp
from jax import lax
from jax.experimental import pallas as pl
from jax.experimental.pallas import tpu as pltpu

F32 = jnp.float32
BF16 = jnp.bfloat16

HEAD = 64
PAIR = 2 * HEAD
NSTATE = 128
SSD_CHUNK = 128
RWKV_CHUNK = 64
SMALL_W = 512
SSM_NORM_EPS = 1e-5
RWKV_GN_EPS = 64e-5
L2_EPS = 1e-12
LN_EPS = 1e-5
VMEM_LIMIT = 56 * 1024 * 1024
SSD_PIECES = 1
RWKV_PIECES = 1
LORA_PIECES = 1
SUM_PIECES = 2
RWKV_UNROLL = 16

_DIMS = {
    "nn": (((1,), (0,)), ((), ())),
    "nt": (((1,), (1,)), ((), ())),
    "tn": (((0,), (0,)), ((), ())),
}


def _split(x, n):
    if x.dtype == BF16:
        return [x]
    parts = []
    rem = x
    for i in range(n):
        p = rem.astype(BF16)
        parts.append(p)
        if i + 1 < n:
            rem = rem - p.astype(F32)
    return parts


def _mm(a, b, dims="nn", pa=1, pb=1):
    aps = _split(a, pa)
    bps = _split(b, pb)
    lim = max(len(aps), len(bps))
    out = None
    for i in reversed(range(len(aps))):
        for j in reversed(range(len(bps))):
            if i + j < lim:
                t = lax.dot_general(aps[i], bps[j], _DIMS[dims], preferred_element_type=F32)
                out = t if out is None else out + t
    return out


def _softplus(x):
    return jnp.maximum(x, 0.0) + jnp.log(1.0 + jnp.exp(-jnp.abs(x)))


def _shift_rows(cur, prev8, j):
    rc = pltpu.roll(cur, j, 0)
    rp = pltpu.roll(prev8, j, 0)
    row = lax.broadcasted_iota(jnp.int32, rp.shape, 0)
    first = jnp.where(row < j, rp, rc[0:8, :])
    return jnp.concatenate([first, rc[8:, :]], axis=0)


def _block_diag(x, m0):
    return jnp.concatenate([jnp.where(m0, x, 0.0), jnp.where(m0, 0.0, x)], axis=0)


def _inproj_kernel(x_ref, w_ref, o_ref, xb_ref):
    @pl.when(pl.program_id(1) == 0)
    def _():
        xb_ref[...] = x_ref[...].astype(BF16)

    o_ref[...] = jnp.dot(xb_ref[...], w_ref[...], preferred_element_type=F32)


def _inproj(x2d, w_bf16, tm, tn):
    T, D = x2d.shape
    N = w_bf16.shape[1]
    return pl.pallas_call(
        _inproj_kernel,
        out_shape=jax.ShapeDtypeStruct((T, N), F32),
        grid=(T // tm, N // tn),
        in_specs=[
            pl.BlockSpec((tm, D), lambda i, j: (i, 0)),
            pl.BlockSpec((D, tn), lambda i, j: (0, j)),
        ],
        out_specs=pl.BlockSpec((tm, tn), lambda i, j: (i, j)),
        scratch_shapes=[pltpu.VMEM((tm, D), BF16)],
        compiler_params=pltpu.CompilerParams(
            dimension_semantics=("arbitrary", "arbitrary"), vmem_limit_bytes=VMEM_LIMIT),
        name="inproj",
    )(x2d, w_bf16)


def _conv_silu(cur_ref, prev_ref, w_ref, b_ref):
    cur = cur_ref[...]
    prev8 = prev_ref[...]
    q = cur.shape[0]
    acc = b_ref[...] + w_ref[3:4, :] * cur
    for j in (1, 2, 3):
        acc = acc + w_ref[3 - j:4 - j, :] * _shift_rows(cur, prev8, j)
    prev_ref[...] = cur[q - 8:, :]
    return acc * jax.nn.sigmoid(acc)


def _ssd_kernel(z_ref, xs_ref, bc_ref, sm_ref, cwx_ref, cbx_ref, cwbc_ref, cbbc_ref,
                dtb_ref, alog_ref, dsk_ref, ng_ref, e_ref, o_ref,
                st_ref, px_ref, pbc_ref, xc_ref, bcc_ref, ex_ref, *, G, H, dt_off, P):
    Q = xs_ref.shape[0]
    N = NSTATE
    GW = 4 * HEAD

    @pl.when(pl.program_id(1) == 0)
    def _():
        st_ref[...] = jnp.zeros_like(st_ref)
        px_ref[...] = jnp.zeros_like(px_ref)
        pbc_ref[...] = jnp.zeros_like(pbc_ref)

    xc_ref[...] = _conv_silu(xs_ref, px_ref, cwx_ref, cbx_ref)
    bcc_ref[...] = _conv_silu(bc_ref, pbc_ref, cwbc_ref, cbbc_ref)

    tile = dt_off // 128
    dt_lane = dt_off % 128
    lane = lax.broadcasted_iota(jnp.int32, (1, 128), 1)
    hmask = (lane >= dt_lane) & (lane < dt_lane + H)
    raw = sm_ref[:, tile * 128:(tile + 1) * 128]
    dt = jnp.where(hmask, _softplus(raw + dtb_ref[...]), 0.0)
    a_neg = jnp.where(hmask, -jnp.exp(alog_ref[...]), 0.0)
    a = dt * a_neg
    row = lax.broadcasted_iota(jnp.int32, (Q, Q), 0)
    col = lax.broadcasted_iota(jnp.int32, (Q, Q), 1)
    causal = col <= row
    tril = jnp.where(causal, 1.0, 0.0).astype(BF16)
    a_cs = _mm(tril, a, "nn", 1, 3)
    a_last = a_cs[Q - 1:Q, :]
    a_cs_t = a_cs.T
    stack = jnp.concatenate(
        [dt, jnp.exp(a_cs), jnp.exp(a_last - a_cs), jnp.broadcast_to(jnp.exp(a_last), (8, 128))], axis=0)
    ex_ref[...] = _mm(stack, e_ref[...], "nn", 3, 1)

    lane_p = lax.broadcasted_iota(jnp.int32, (1, PAIR), 1)
    m0 = lane_p < HEAD
    for g in range(G):
        gs = slice(g * GW, (g + 1) * GW)
        bg = bcc_ref[:, g * N:(g + 1) * N]
        cg = bcc_ref[:, G * N + g * N:G * N + (g + 1) * N]
        scores = _mm(cg, bg, "nt", P, P)
        xsg = xc_ref[:, gs]
        xdt = xsg * ex_ref[0:Q, gs]
        st = st_ref[g]
        y = _mm(cg, st, "nn", P, P) * ex_ref[Q:2 * Q, gs]
        ypairs = []
        for pr in range(2):
            xp = xdt[:, pr * PAIR:(pr + 1) * PAIR]
            acc = None
            for half in range(2):
                hl = dt_lane + g * 4 + pr * 2 + half
                seg = a_cs[:, hl:hl + 1] - a_cs_t[hl:hl + 1, :]
                lm = jnp.exp(jnp.where(causal, seg, -jnp.inf))
                xm = jnp.where(m0, xp, 0.0) if half == 0 else jnp.where(m0, 0.0, xp)
                t = _mm(scores * lm, xm, "nn", P, P)
                acc = t if acc is None else acc + t
            ypairs.append(acc)
        y = y + jnp.concatenate(ypairs, axis=1)
        st_ref[g] = st * ex_ref[3 * Q:3 * Q + 1, gs] + _mm(bg.T, xdt * ex_ref[2 * Q:3 * Q, gs], "nn", P, P)
        y = y + dsk_ref[:, gs] * xsg
        zg = z_ref[:, gs]
        v = y * (zg * jax.nn.sigmoid(zg))
        ms = jnp.mean(v * v, axis=-1, keepdims=True)
        o_ref[:, gs] = (v * lax.rsqrt(ms + SSM_NORM_EPS) * ng_ref[:, gs]).astype(BF16)


def _ssd(u, B, L, D_SSM, G, H, dt_off, small_blk, cwx, cbx, cwbc, cbbc, dtb, alog, dsk, ng, emat, P):
    Q = SSD_CHUNK
    NC = L // Q
    T = B * L
    full = lambda shape: pl.BlockSpec(shape, lambda b, c: (0,) * len(shape))
    rowblk = lambda w, j: pl.BlockSpec((Q, w), lambda b, c: (b * NC + c, j))
    return pl.pallas_call(
        functools.partial(_ssd_kernel, G=G, H=H, dt_off=dt_off, P=P),
        out_shape=jax.ShapeDtypeStruct((T, D_SSM), BF16),
        grid=(B, NC),
        in_specs=[
            rowblk(D_SSM, 0), rowblk(D_SSM, 1), rowblk(D_SSM, 2), rowblk(SMALL_W, small_blk),
            full(cwx.shape), full(cbx.shape), full(cwbc.shape), full(cbbc.shape),
            full(dtb.shape), full(alog.shape), full(dsk.shape), full(ng.shape), full(emat.shape),
        ],
        out_specs=pl.BlockSpec((Q, D_SSM), lambda b, c: (b * NC + c, 0)),
        scratch_shapes=[
            pltpu.VMEM((G, NSTATE, 4 * HEAD), F32),
            pltpu.VMEM((8, D_SSM), F32),
            pltpu.VMEM((8, D_SSM), F32),
            pltpu.VMEM((Q, D_SSM), F32),
            pltpu.VMEM((Q, D_SSM), F32),
            pltpu.VMEM((3 * Q + 8, D_SSM), F32),
        ],
        compiler_params=pltpu.CompilerParams(
            dimension_semantics=("arbitrary", "arbitrary"), vmem_limit_bytes=VMEM_LIMIT),
        name="ssd",
    )(u, u, u, u, cwx, cbx, cwbc, cbbc, dtb, alog, dsk, ng, emat)


def _rwkv_kernel(r_ref, k_ref, v_ref, sm_ref, mur_ref, muk_ref, muv_ref, mus_ref,
                 w0_ref, a0_ref, kk_ref, ka_ref, rk_ref, gng_ref, gnb_ref,
                 wd_ref, wa_ref, wg_ref, o_ref,
                 st_ref, pr_ref, pk_ref, pv_ref, ps_ref, th_ref, xa_ref, sg_ref, *, NP, UNR, P, PL):
    C = r_ref.shape[0]

    @pl.when(pl.program_id(1) == 0)
    def _():
        st_ref[...] = jnp.zeros_like(st_ref)
        pr_ref[...] = jnp.zeros_like(pr_ref)
        pk_ref[...] = jnp.zeros_like(pk_ref)
        pv_ref[...] = jnp.zeros_like(pv_ref)
        ps_ref[...] = jnp.zeros_like(ps_ref)

    def shift_lerp(cur, prev8, mu):
        prev = _shift_rows(cur, prev8, 1)
        return cur + (prev - cur) * mu

    sm = sm_ref[...]
    ss = shift_lerp(sm, ps_ref[...], mus_ref[...])
    ps_ref[...] = sm[C - 8:, :]
    th_ref[...] = jnp.tanh(ss[:, 0:128])
    xa_ref[...] = ss[:, 0:256]
    sg_ref[...] = jax.nn.sigmoid(ss[:, 128:512])

    lane = lax.broadcasted_iota(jnp.int32, (C, PAIR), 1)
    rowi = lax.broadcasted_iota(jnp.int32, (C, PAIR), 0)
    m0 = lane < HEAD
    s_idx = jnp.where(m0, lane, lane - HEAD)
    strict = s_idx < rowi
    incl = s_idx <= rowi
    eye_p = jnp.where(s_idx == rowi, 1.0, 0.0)
    r2 = lax.broadcasted_iota(jnp.int32, (PAIR, PAIR), 0)
    c2 = lax.broadcasted_iota(jnp.int32, (PAIR, PAIR), 1)
    ones_bd = jnp.where((r2 < HEAD) == (c2 < HEAD), 1.0, 0.0).astype(BF16)
    r3 = lax.broadcasted_iota(jnp.int32, (C, C), 0)
    c3 = lax.broadcasted_iota(jnp.int32, (C, C), 1)
    tril = jnp.where(c3 <= r3, 1.0, 0.0).astype(BF16)
    n_dbl = C.bit_length() - 2

    def one_pair(ds):
        r_raw = r_ref[:, ds]
        k_raw = k_ref[:, ds]
        v_raw = v_ref[:, ds]
        s_r = shift_lerp(r_raw, pr_ref[:, ds], mur_ref[:, ds])
        s_k = shift_lerp(k_raw, pk_ref[:, ds], muk_ref[:, ds])
        v = shift_lerp(v_raw, pv_ref[:, ds], muv_ref[:, ds])
        s0 = st_ref[:, ds]
        w0, a0, ka, rk = w0_ref[:, ds], a0_ref[:, ds], ka_ref[:, ds], rk_ref[:, ds]
        gng, gnb = gng_ref[:, ds], gnb_ref[:, ds]
        kkr = s_k * kk_ref[:, ds]
        lw = _mm(th_ref[...], wd_ref[:, ds], "nn", PL, PL)
        la = _mm(xa_ref[...], wa_ref[:, ds], "nn", PL, PL)
        gate = _mm(sg_ref[...], wg_ref[:, ds], "nn", PL, PL)
        n2 = _mm(kkr * kkr, ones_bd, "nn", SUM_PIECES, 1)
        bdv = _block_diag(v, m0)
        yield
        ld = -jnp.exp(-_softplus(-(w0 + lw)) - 0.5)
        a_sig = jax.nn.sigmoid(a0 + la)
        kk = kkr / jnp.maximum(jnp.sqrt(n2), L2_EPS)
        k2 = s_k * (1.0 + (a_sig - 1.0) * ka)
        bvec = kk * a_sig
        cs = _mm(tril, ld, "nn", 1, 3)
        bonus = _mm(s_r * k2 * rk, ones_bd, "nn", SUM_PIECES, 1)
        yield
        cs_last = cs[C - 1:C, :]
        e_in = jnp.exp(-cs)
        e_end = jnp.exp(cs_last - cs)
        at = -kk * jnp.exp(cs - ld)
        rt = s_r * jnp.exp(cs)
        lhs = jnp.concatenate([at, rt], axis=0)
        ak = _mm(lhs, _block_diag(k2 * e_in, m0), "nt", P, P)
        ab = _mm(lhs, _block_diag(bvec * e_in, m0), "nt", P, P)
        ss0 = _mm(lhs, _block_diag(s0, m0), "nt", P, P)
        kb = jnp.concatenate([k2 * e_end, bvec * e_end], axis=0)
        yield
        a_ak = jnp.where(strict, ak[0:C], 0.0)
        a_rk = jnp.where(incl, ak[C:2 * C], 0.0)
        a_rb = jnp.where(incl, ab[C:2 * C], 0.0)
        m = jnp.where(strict, ab[0:C], 0.0)
        tinv = eye_p + m
        m = _mm(m, _block_diag(m, m0), "nn", P, P)
        ru = ss0[0:C] + _mm(a_ak, bdv, "nn", P, P)
        y0 = ss0[C:2 * C] + _mm(a_rk, bdv, "nn", P, P)
        yield
        for _ in range(n_dbl - 1):
            bdm = _block_diag(m, m0)
            tinv = tinv + _mm(tinv, bdm, "nn", P, P)
            m = _mm(m, bdm, "nn", P, P)
            yield
        tinv = tinv + _mm(tinv, _block_diag(m, m0), "nn", P, P)
        yield
        u = _mm(tinv, _block_diag(ru, m0), "nn", P, P)
        yield
        y = y0 + _mm(a_rb, _block_diag(u, m0), "nn", P, P)
        vu = jnp.concatenate([v, u], axis=0)
        dl = _mm(vu.T, kb, "nn", P, P)
        yield
        inv_n = 1.0 / HEAD
        mean = _mm(y, ones_bd, "nn", SUM_PIECES, 1) * inv_n
        s_new = s0 * jnp.exp(cs_last) + jnp.where(m0, dl[0:HEAD], dl[HEAD:PAIR])
        yield
        d = y - mean
        var = _mm(d * d, ones_bd, "nn", SUM_PIECES, 1) * inv_n
        yield
        yn = d * lax.rsqrt(var + RWKV_GN_EPS) * gng + gnb
        out = ((yn + bonus * v) * gate).astype(BF16)
        yield
        pr_ref[:, ds] = r_raw[C - 8:, :]
        pk_ref[:, ds] = k_raw[C - 8:, :]
        pv_ref[:, ds] = v_raw[C - 8:, :]
        st_ref[:, ds] = s_new
        o_ref[:, ds] = out

    def body(i, carry):
        gens = [one_pair(pl.ds(pl.multiple_of((i * UNR + k) * PAIR, PAIR), PAIR)) for k in range(UNR)]
        while gens:
            gens = [g for g in gens if next(g, "done") != "done"]
        return carry

    lax.fori_loop(0, NP // UNR, body, 0)


def _rwkv(u, B, L, D_R, rkv_blk, small_blk, params, P, PL, UNR):
    C = RWKV_CHUNK
    NC = L // C
    T = B * L
    NP = D_R // PAIR
    full = lambda a: pl.BlockSpec(a.shape, lambda b, c: (0,) * a.ndim)
    rowblk = lambda w, j: pl.BlockSpec((C, w), lambda b, c: (b * NC + c, j))
    return pl.pallas_call(
        functools.partial(_rwkv_kernel, NP=NP, UNR=UNR, P=P, PL=PL),
        out_shape=jax.ShapeDtypeStruct((T, D_R), BF16),
        grid=(B, NC),
        in_specs=[rowblk(D_R, rkv_blk), rowblk(D_R, rkv_blk + 1), rowblk(D_R, rkv_blk + 2),
                  rowblk(SMALL_W, small_blk)] + [full(a) for a in params],
        out_specs=pl.BlockSpec((C, D_R), lambda b, c: (b * NC + c, 0)),
        scratch_shapes=[
            pltpu.VMEM((HEAD, D_R), F32),
            pltpu.VMEM((8, D_R), F32), pltpu.VMEM((8, D_R), F32), pltpu.VMEM((8, D_R), F32),
            pltpu.VMEM((8, SMALL_W), F32),
            pltpu.VMEM((C, 128), F32), pltpu.VMEM((C, 256), F32), pltpu.VMEM((C, 384), F32),
        ],
        compiler_params=pltpu.CompilerParams(
            dimension_semantics=("arbitrary", "arbitrary"), vmem_limit_bytes=VMEM_LIMIT),
        name="rwkv",
    )(u, u, u, u, *params)


LN_ROWS = 64


def _layer_norm_rows(o_ref, g_ref, b_ref):
    tm = o_ref.shape[0]

    def body(i, carry):
        rs = pl.ds(pl.multiple_of(i * LN_ROWS, LN_ROWS), LN_ROWS)
        h = o_ref[rs, :]
        mu = jnp.mean(h, axis=-1, keepdims=True)
        d = h - mu
        var = jnp.mean(d * d, axis=-1, keepdims=True)
        o_ref[rs, :] = d * lax.rsqrt(var + LN_EPS) * g_ref[...] + b_ref[...]
        return carry

    lax.fori_loop(0, tm // LN_ROWS, body, 0)


def _outproj_ln_kernel(y1_ref, y2_ref, w1_ref, w2_ref, res_ref, g_ref, b_ref, o_ref, *, alpha, tn, nj):
    j = pl.program_id(1)
    acc = jnp.dot(y1_ref[...], w1_ref[...], preferred_element_type=F32)
    acc = acc + jnp.dot(y2_ref[...], w2_ref[...], preferred_element_type=F32)
    cs = pl.ds(pl.multiple_of(j * tn, tn), tn)
    o_ref[:, cs] = alpha * res_ref[...] + acc

    @pl.when(j == nj - 1)
    def _():
        _layer_norm_rows(o_ref, g_ref, b_ref)


def _outproj_ln(y1, y2, w1, w2, res, g, b, alpha, tm, tn):
    T, K1 = y1.shape
    K2 = y2.shape[1]
    D = w1.shape[1]
    nj = D // tn
    return pl.pallas_call(
        functools.partial(_outproj_ln_kernel, alpha=alpha, tn=tn, nj=nj),
        out_shape=jax.ShapeDtypeStruct((T, D), F32),
        grid=(T // tm, nj),
        in_specs=[
            pl.BlockSpec((tm, K1), lambda i, j: (i, 0)),
            pl.BlockSpec((tm, K2), lambda i, j: (i, 0)),
            pl.BlockSpec((K1, tn), lambda i, j: (0, j)),
            pl.BlockSpec((K2, tn), lambda i, j: (0, j)),
            pl.BlockSpec((tm, tn), lambda i, j: (i, j)),
            pl.BlockSpec((1, D), lambda i, j: (0, 0)),
            pl.BlockSpec((1, D), lambda i, j: (0, 0)),
        ],
        out_specs=pl.BlockSpec((tm, D), lambda i, j: (i, 0)),
        compiler_params=pltpu.CompilerParams(
            dimension_semantics=("arbitrary", "arbitrary"), vmem_limit_bytes=VMEM_LIMIT),
        name="outproj_ln",
    )(y1, y2, w1, w2, res, g, b)


def _ffn_ln_kernel(h_ref, wu_ref, wd_ref, g_ref, b_ref, o_ref, hb_ref, *, alpha, nf):
    f = pl.program_id(1)

    @pl.when(f == 0)
    def _():
        h = h_ref[...]
        hb_ref[...] = h.astype(BF16)
        o_ref[...] = alpha * h

    a = jnp.dot(hb_ref[...], wu_ref[...], preferred_element_type=F32)
    a = jnp.maximum(a, 0.0)
    a = (a * a).astype(BF16)
    o_ref[...] += jnp.dot(a, wd_ref[...], preferred_element_type=F32)

    @pl.when(f == nf - 1)
    def _():
        _layer_norm_rows(o_ref, g_ref, b_ref)


def _ffn_ln(h, wu, wd, g, b, alpha, tm, tf):
    T, D = h.shape
    F = wu.shape[1]
    nf = F // tf
    return pl.pallas_call(
        functools.partial(_ffn_ln_kernel, alpha=alpha, nf=nf),
        out_shape=jax.ShapeDtypeStruct((T, D), F32),
        grid=(T // tm, nf),
        in_specs=[
            pl.BlockSpec((tm, D), lambda i, f: (i, 0), pipeline_mode=pl.Buffered(1)),
            pl.BlockSpec((D, tf), lambda i, f: (0, f)),
            pl.BlockSpec((tf, D), lambda i, f: (f, 0)),
            pl.BlockSpec((1, D), lambda i, f: (0, 0)),
            pl.BlockSpec((1, D), lambda i, f: (0, 0)),
        ],
        out_specs=pl.BlockSpec((tm, D), lambda i, f: (i, 0)),
        scratch_shapes=[pltpu.VMEM((tm, D), BF16)],
        compiler_params=pltpu.CompilerParams(
            dimension_semantics=("arbitrary", "arbitrary"), vmem_limit_bytes=VMEM_LIMIT),
        name="ffn_ln",
    )(h, wu, wd, g, b)


def _ple_ln_kernel(h_ref, p_ref, wg_ref, wp_ref, g_ref, b_ref, o_ref, hb_ref, *, alpha, tn, nj):
    j = pl.program_id(1)

    @pl.when(j == 0)
    def _():
        hb_ref[...] = h_ref[...].astype(BF16)

    gate = jax.nn.sigmoid(jnp.dot(hb_ref[...], wg_ref[...], preferred_element_type=F32))
    emb = jnp.dot(p_ref[...].astype(BF16), wp_ref[...], preferred_element_type=F32)
    cs = pl.ds(pl.multiple_of(j * tn, tn), tn)
    o_ref[:, cs] = alpha * h_ref[:, cs] + emb * gate

    @pl.when(j == nj - 1)
    def _():
        _layer_norm_rows(o_ref, g_ref, b_ref)


def _ple_ln(h, p2d, wg, wp, g, b, alpha, tm, tn):
    T, D = h.shape
    DP = p2d.shape[1]
    nj = D // tn
    return pl.pallas_call(
        functools.partial(_ple_ln_kernel, alpha=alpha, tn=tn, nj=nj),
        out_shape=jax.ShapeDtypeStruct((T, D), F32),
        grid=(T // tm, nj),
        in_specs=[
            pl.BlockSpec((tm, D), lambda i, j: (i, 0), pipeline_mode=pl.Buffered(1)),
            pl.BlockSpec((tm, DP), lambda i, j: (i, 0)),
            pl.BlockSpec((D, tn), lambda i, j: (0, j)),
            pl.BlockSpec((DP, tn), lambda i, j: (0, j)),
            pl.BlockSpec((1, D), lambda i, j: (0, 0)),
            pl.BlockSpec((1, D), lambda i, j: (0, 0)),
        ],
        out_specs=pl.BlockSpec((tm, D), lambda i, j: (i, 0)),
        scratch_shapes=[pltpu.VMEM((tm, D), BF16)],
        compiler_params=pltpu.CompilerParams(
            dimension_semantics=("arbitrary", "arbitrary"), vmem_limit_bytes=VMEM_LIMIT),
        name="ple_ln",
    )(h, p2d, wg, wp, g, b)


def _pad_rows(w, rows, offset=0):
    out = jnp.zeros((rows, w.shape[1]), w.dtype)
    return out.at[offset:offset + w.shape[0]].set(w)


def _layer(h, p_i, w_in, conv_w, conv_b, dt_bias, A_log, D_skip, ssm_norm_g,
           rwkv_mu, w0, w_decay_b, a0, w_aaa_b, w_gate_b, k_k, k_a, r_k, gn_g, gn_b,
           w_out, ln1_g, ln1_b, w_up, w_down, ln2_g, ln2_b, w_ple, w_ple_gate, ln3_g, ln3_b, alpha):
    B, L, D = h.shape
    T = B * L
    D_SSM = ssm_norm_g.shape[0]
    H = dt_bias.shape[0]
    D_XBC = conv_w.shape[1]
    G = (D_XBC - D_SSM) // (2 * NSTATE)
    D_R = w0.shape[0]
    dl, al, gl = w_decay_b.shape[0], w_aaa_b.shape[0], w_gate_b.shape[0]
    dt_off = dl + al + gl
    assert D_XBC == 2 * D_SSM and D_R == D_SSM and D_SSM == H * HEAD and H == 4 * G
    assert dl <= 128 and dl + al <= 256 and 128 <= dl + al and dt_off + H <= SMALL_W
    assert dt_off // 128 == (dt_off + H - 1) // 128
    assert L % SSD_CHUNK == 0 and L % RWKV_CHUNK == 0 and D_R % PAIR == 0

    o_xbc = D_SSM
    o_dt = D_SSM + D_XBC
    o_r = o_dt + H
    o_lo = o_r + 3 * D_R
    pad = SMALL_W - (dt_off + H)
    w_cat = jnp.concatenate(
        [w_in[:, :o_dt], w_in[:, o_r:o_lo], w_in[:, o_lo:], w_in[:, o_dt:o_r], jnp.zeros((D, pad), w_in.dtype)],
        axis=1).astype(BF16)
    small_blk = (D_SSM + D_XBC + 3 * D_R) // SMALL_W
    rkv_blk = (D_SSM + D_XBC) // D_R

    tm = min(512, T)
    n_cat = w_cat.shape[1]
    tn_in = next(t for t in (1280, 1024, 512) if n_cat % t == 0)
    tn = min(1024, D)
    u = _inproj(h.reshape(T, D), w_cat, tm, tn_in)

    row = lambda a: a.reshape(1, -1).astype(F32)
    lane_tile = lambda a: jnp.zeros((1, 128), F32).at[0, dt_off % 128:dt_off % 128 + H].set(a)
    hl = jnp.arange(128)[:, None] - dt_off % 128
    emat = (hl == (jnp.arange(D_SSM)[None, :] // HEAD)).astype(BF16)
    y_ssm = _ssd(u, B, L, D_SSM, G, H, dt_off, small_blk,
                 conv_w[:, :D_SSM], row(conv_b[:D_SSM]), conv_w[:, D_SSM:], row(conv_b[D_SSM:]),
                 lane_tile(dt_bias), lane_tile(A_log), row(jnp.repeat(D_skip, HEAD)), row(ssm_norm_g), emat, P=SSD_PIECES)

    mu = rwkv_mu
    mus = jnp.concatenate([mu[3 * D_R:], jnp.zeros((SMALL_W - dt_off,), F32)])
    params = [row(mu[:D_R]), row(mu[D_R:2 * D_R]), row(mu[2 * D_R:3 * D_R]), row(mus),
              row(w0), row(a0), row(k_k), row(k_a), row(r_k), row(gn_g), row(gn_b),
              _pad_rows(w_decay_b, 128), _pad_rows(w_aaa_b, 256, dl), _pad_rows(w_gate_b, 384, dl + al - 128)]
    y_rwkv = _rwkv(u, B, L, D_R, rkv_blk, small_blk, params, P=RWKV_PIECES, PL=LORA_PIECES,
                   UNR=min(RWKV_UNROLL, D_R // PAIR))

    wo = w_out.astype(BF16)
    h1 = _outproj_ln(y_ssm, y_rwkv, wo[:D_SSM], wo[D_SSM:], h.reshape(T, D), row(ln1_g), row(ln1_b), alpha, tm, tn)
    h2 = _ffn_ln(h1, w_up.astype(BF16), w_down.astype(BF16), row(ln2_g), row(ln2_b), alpha, tm, 512)
    h3 = _ple_ln(h2, p_i.reshape(T, -1), w_ple_gate.astype(BF16), w_ple.astype(BF16),
                 row(ln3_g), row(ln3_b), alpha, tm, tn)
    return h3.reshape(B, L, D)


def kernel(x, p, w_in, conv_w, conv_b, dt_bias, A_log, D_skip, ssm_norm_g, rwkv_mu, w0, w_decay_b, a0, w_aaa_b, w_gate_b, k_k, k_a, r_k, gn_g, gn_b, w_out, ln1_g, ln1_b, w_up, w_down, ln2_g, ln2_b, w_ple, w_ple_gate, ln3_g, ln3_b):
    depth = w_in.shape[0]
    alpha = float((2 * depth) ** 0.25)
    h = x
    for i in range(depth):
        h = _layer(h, p[i], w_in[i], conv_w[i], conv_b[i], dt_bias[i], A_log[i], D_skip[i], ssm_norm_g[i],
                   rwkv_mu[i], w0[i], w_decay_b[i], a0[i], w_aaa_b[i], w_gate_b[i], k_k[i], k_a[i],
                   r_k[i].reshape(-1), gn_g[i], gn_b[i], w_out[i], ln1_g[i], ln1_b[i], w_up[i], w_down[i],
                   ln2_g[i], ln2_b[i], w_ple[i], w_ple_gate[i], ln3_g[i], ln3_b[i], alpha)
    return h
```

```python
import functools

import jax
import jax.numpy as jnp
from jax import lax
from jax.experimental import pallas as pl
from jax.experimental.pallas import tpu as pltpu

F32 = jnp.float32
BF16 = jnp.bfloat16

HEAD = 64
PAIR = 2 * HEAD
NSTATE = 128
SSD_CHUNK = 128
RWKV_CHUNK = 64
SMALL_W = 512
SSM_NORM_EPS = 1e-5
RWKV_GN_EPS = 64e-5
L2_EPS = 1e-12
LN_EPS = 1e-5
VMEM_LIMIT = 60 * 1024 * 1024
SSD_PIECES = 1
RWKV_PIECES = 1
LORA_PIECES = 1
SUM_PIECES = 2
CUMSUM_PIECES = 3
RWKV_UNROLL = 16

_DIMS = {
    "nn": (((1,), (0,)), ((), ())),
    "nt": (((1,), (1,)), ((), ())),
    "tn": (((0,), (0,)), ((), ())),
}


def _split(x, n):
    if x.dtype == BF16:
        return [x]
    parts = []
    rem = x
    for i in range(n):
        p = rem.astype(BF16)
        parts.append(p)
        if i + 1 < n:
            rem = rem - p.astype(F32)
    return parts


def _mm(a, b, dims="nn", pa=1, pb=1):
    aps = _split(a, pa)
    bps = _split(b, pb)
    lim = max(len(aps), len(bps))
    out = None
    for i in reversed(range(len(aps))):
        for j in reversed(range(len(bps))):
            if i + j < lim:
                t = lax.dot_general(aps[i], bps[j], _DIMS[dims], preferred_element_type=F32)
                out = t if out is None else out + t
    return out


def _softplus(x):
    return jnp.maximum(x, 0.0) + jnp.log(1.0 + jnp.exp(-jnp.abs(x)))


def _shift_rows(cur, prev8, j):
    rc = pltpu.roll(cur, j, 0)
    rp = pltpu.roll(prev8, j, 0)
    row = lax.broadcasted_iota(jnp.int32, rp.shape, 0)
    first = jnp.where(row < j, rp, rc[0:8, :])
    return jnp.concatenate([first, rc[8:, :]], axis=0)


def _block_diag(x, m0):
    return jnp.concatenate([jnp.where(m0, x, 0.0), jnp.where(m0, 0.0, x)], axis=0)


PREP_ROWS = 256
PREP_CHUNK = 1024


def _reorder_cols_kernel(w_ref, o_ref, *, pieces):
    n_src = w_ref.shape[1]
    rows = w_ref.shape[0]

    def src_cols(s, wd):
        lo = s - s % 128
        lw = min(-(-(s % 128 + wd) // 128) * 128, n_src - lo)
        return w_ref[:, lo:lo + lw][:, s % 128:s % 128 + wd]

    n_out = o_ref.shape[1]
    for c0 in range(0, n_out, PREP_CHUNK):
        c1 = min(c0 + PREP_CHUNK, n_out)
        parts = []
        pos = 0
        for s, wd in pieces:
            a, b = max(pos, c0), min(pos + wd, c1)
            if a < b:
                parts.append(jnp.zeros((rows, b - a), F32) if s is None else src_cols(s + a - pos, b - a))
            pos += wd
        chunk = parts[0] if len(parts) == 1 else jnp.concatenate(parts, axis=1)
        o_ref[:, c0:c1] = chunk.astype(BF16)


def _reorder_cols(w, pieces):
    K, n_src = w.shape
    n_out = sum(wd for _, wd in pieces)
    rb = min(PREP_ROWS, K)
    return pl.pallas_call(
        functools.partial(_reorder_cols_kernel, pieces=tuple(pieces)),
        out_shape=jax.ShapeDtypeStruct((K, n_out), BF16),
        grid=(K // rb,),
        in_specs=[pl.BlockSpec((rb, n_src), lambda i: (i, 0))],
        out_specs=pl.BlockSpec((rb, n_out), lambda i: (i, 0)),
        compiler_params=pltpu.CompilerParams(dimension_semantics=("arbitrary",), vmem_limit_bytes=VMEM_LIMIT),
        name="reorder_cols",
    )(w)


def _inproj_kernel(x_ref, w_ref, o_ref, xb_ref):
    @pl.when(pl.program_id(1) == 0)
    def _():
        xb_ref[...] = x_ref[...].astype(BF16)

    o_ref[...] = jnp.dot(xb_ref[...], w_ref[...], preferred_element_type=F32)


def _inproj(x2d, w_bf16, tm, tn):
    T, D = x2d.shape
    N = w_bf16.shape[1]
    return pl.pallas_call(
        _inproj_kernel,
        out_shape=jax.ShapeDtypeStruct((T, N), F32),
        grid=(T // tm, N // tn),
        in_specs=[
            pl.BlockSpec((tm, D), lambda i, j: (i, 0)),
            pl.BlockSpec((D, tn), lambda i, j: (0, j)),
        ],
        out_specs=pl.BlockSpec((tm, tn), lambda i, j: (i, j)),
        scratch_shapes=[pltpu.VMEM((tm, D), BF16)],
        compiler_params=pltpu.CompilerParams(
            dimension_semantics=("arbitrary", "arbitrary"), vmem_limit_bytes=VMEM_LIMIT),
        name="inproj",
    )(x2d, w_bf16)


def _conv_silu(cur_ref, prev_ref, w_ref, b_ref):
    cur = cur_ref[...]
    prev8 = prev_ref[...]
    q = cur.shape[0]
    acc = b_ref[...] + w_ref[3:4, :] * cur
    for j in (1, 2, 3):
        acc = acc + w_ref[3 - j:4 - j, :] * _shift_rows(cur, prev8, j)
    prev_ref[...] = cur[q - 8:, :]
    return acc * jax.nn.sigmoid(acc)


def _ssd_kernel(z_ref, xs_ref, bc_ref, sm_ref, cwx_ref, cbx_ref, cwbc_ref, cbbc_ref,
                dtb_ref, alog_ref, dsk_ref, ng_ref, e_ref, o_ref,
                st_ref, px_ref, pbc_ref, xc_ref, bcc_ref, ex_ref, *, G, H, dt_off, P):
    Q = xs_ref.shape[0]
    N = NSTATE
    GW = 4 * HEAD

    @pl.when(pl.program_id(1) == 0)
    def _():
        st_ref[...] = jnp.zeros_like(st_ref)
        px_ref[...] = jnp.zeros_like(px_ref)
        pbc_ref[...] = jnp.zeros_like(pbc_ref)

    xc_ref[...] = _conv_silu(xs_ref, px_ref, cwx_ref, cbx_ref)
    bcc_ref[...] = _conv_silu(bc_ref, pbc_ref, cwbc_ref, cbbc_ref)

    tile = dt_off // 128
    dt_lane = dt_off % 128
    lane = lax.broadcasted_iota(jnp.int32, (1, 128), 1)
    hmask = (lane >= dt_lane) & (lane < dt_lane + H)
    raw = sm_ref[:, tile * 128:(tile + 1) * 128]
    dt = jnp.where(hmask, _softplus(raw + dtb_ref[...]), 0.0)
    a_neg = jnp.where(hmask, -jnp.exp(alog_ref[...]), 0.0)
    a = dt * a_neg
    row = lax.broadcasted_iota(jnp.int32, (Q, Q), 0)
    col = lax.broadcasted_iota(jnp.int32, (Q, Q), 1)
    causal = col <= row
    tril = jnp.where(causal, 1.0, 0.0).astype(BF16)
    a_cs = _mm(tril, a, "nn", 1, 3)
    a_last = a_cs[Q - 1:Q, :]
    a_cs_t = a_cs.T
    stack = jnp.concatenate(
        [dt, jnp.exp(a_cs), jnp.exp(a_last - a_cs), jnp.broadcast_to(jnp.exp(a_last), (8, 128))], axis=0)
    ex_ref[...] = _mm(stack, e_ref[...], "nn", 3, 1)

    lane_p = lax.broadcasted_iota(jnp.int32, (1, PAIR), 1)
    m0 = lane_p < HEAD
    for g in range(G):
        gs = slice(g * GW, (g + 1) * GW)
        bg = bcc_ref[:, g * N:(g + 1) * N]
        cg = bcc_ref[:, G * N + g * N:G * N + (g + 1) * N]
        scores = _mm(cg, bg, "nt", P, P)
        xsg = xc_ref[:, gs]
        xdt = xsg * ex_ref[0:Q, gs]
        st = st_ref[g]
        y = _mm(cg, st, "nn", P, P) * ex_ref[Q:2 * Q, gs]
        ypairs = []
        for pr in range(2):
            xp = xdt[:, pr * PAIR:(pr + 1) * PAIR]
            acc = None
            for half in range(2):
                hl = dt_lane + g * 4 + pr * 2 + half
                seg = a_cs[:, hl:hl + 1] - a_cs_t[hl:hl + 1, :]
                lm = jnp.exp(jnp.where(causal, seg, -jnp.inf))
                xm = jnp.where(m0, xp, 0.0) if half == 0 else jnp.where(m0, 0.0, xp)
                t = _mm(scores * lm, xm, "nn", P, P)
                acc = t if acc is None else acc + t
            ypairs.append(acc)
        y = y + jnp.concatenate(ypairs, axis=1)
        st_ref[g] = st * ex_ref[3 * Q:3 * Q + 1, gs] + _mm(bg.T, xdt * ex_ref[2 * Q:3 * Q, gs], "nn", P, P)
        y = y + dsk_ref[:, gs] * xsg
        zg = z_ref[:, gs]
        v = y * (zg * jax.nn.sigmoid(zg))
        ms = jnp.mean(v * v, axis=-1, keepdims=True)
        o_ref[:, gs] = (v * lax.rsqrt(ms + SSM_NORM_EPS) * ng_ref[:, gs]).astype(BF16)


def _ssd(u, B, L, D_SSM, G, H, dt_off, small_blk, cwx, cbx, cwbc, cbbc, dtb, alog, dsk, ng, emat, P):
    Q = SSD_CHUNK
    NC = L // Q
    T = B * L
    full = lambda shape: pl.BlockSpec(shape, lambda b, c: (0,) * len(shape))
    rowblk = lambda w, j: pl.BlockSpec((Q, w), lambda b, c: (b * NC + c, j))
    return pl.pallas_call(
        functools.partial(_ssd_kernel, G=G, H=H, dt_off=dt_off, P=P),
        out_shape=jax.ShapeDtypeStruct((T, D_SSM), BF16),
        grid=(B, NC),
        in_specs=[
            rowblk(D_SSM, 0), rowblk(D_SSM, 1), rowblk(D_SSM, 2), rowblk(SMALL_W, small_blk),
            full(cwx.shape), full(cbx.shape), full(cwbc.shape), full(cbbc.shape),
            full(dtb.shape), full(alog.shape), full(dsk.shape), full(ng.shape), full(emat.shape),
        ],
        out_specs=pl.BlockSpec((Q, D_SSM), lambda b, c: (b * NC + c, 0)),
        scratch_shapes=[
            pltpu.VMEM((G, NSTATE, 4 * HEAD), F32),
            pltpu.VMEM((8, D_SSM), F32),
            pltpu.VMEM((8, D_SSM), F32),
            pltpu.VMEM((Q, D_SSM), F32),
            pltpu.VMEM((Q, D_SSM), F32),
            pltpu.VMEM((3 * Q + 8, D_SSM), F32),
        ],
        compiler_params=pltpu.CompilerParams(
            dimension_semantics=("arbitrary", "arbitrary"), vmem_limit_bytes=VMEM_LIMIT),
        name="ssd",
    )(u, u, u, u, cwx, cbx, cwbc, cbbc, dtb, alog, dsk, ng, emat)


def _rwkv_kernel(r_ref, k_ref, v_ref, sm_ref, mur_ref, muk_ref, muv_ref, mus_ref,
                 w0_ref, a0_ref, kk_ref, ka_ref, rk_ref, gng_ref, gnb_ref,
                 wd_ref, wa_ref, wg_ref, o_ref,
                 st_ref, pr_ref, pk_ref, pv_ref, ps_ref, th_ref, xa_ref, sg_ref, *, NP, UNR, P, PL):
    C = r_ref.shape[0]

    @pl.when(pl.program_id(1) == 0)
    def _():
        st_ref[...] = jnp.zeros_like(st_ref)
        pr_ref[...] = jnp.zeros_like(pr_ref)
        pk_ref[...] = jnp.zeros_like(pk_ref)
        pv_ref[...] = jnp.zeros_like(pv_ref)
        ps_ref[...] = jnp.zeros_like(ps_ref)

    def shift_lerp(cur, prev8, mu):
        prev = _shift_rows(cur, prev8, 1)
        return cur + (prev - cur) * mu

    sm = sm_ref[...]
    ss = shift_lerp(sm, ps_ref[...], mus_ref[...])
    ps_ref[...] = sm[C - 8:, :]
    th_ref[...] = jnp.tanh(ss[:, 0:128])
    xa_ref[...] = ss[:, 0:256]
    sg_ref[...] = jax.nn.sigmoid(ss[:, 128:512])

    lane = lax.broadcasted_iota(jnp.int32, (C, PAIR), 1)
    rowi = lax.broadcasted_iota(jnp.int32, (C, PAIR), 0)
    m0 = lane < HEAD
    s_idx = jnp.where(m0, lane, lane - HEAD)
    strict = s_idx < rowi
    incl = s_idx <= rowi
    eye_p = jnp.where(s_idx == rowi, 1.0, 0.0)
    r2 = lax.broadcasted_iota(jnp.int32, (PAIR, PAIR), 0)
    c2 = lax.broadcasted_iota(jnp.int32, (PAIR, PAIR), 1)
    ones_bd = jnp.where((r2 < HEAD) == (c2 < HEAD), 1.0, 0.0).astype(BF16)
    r3 = lax.broadcasted_iota(jnp.int32, (C, C), 0)
    c3 = lax.broadcasted_iota(jnp.int32, (C, C), 1)
    tril = jnp.where(c3 <= r3, 1.0, 0.0).astype(BF16)
    n_dbl = C.bit_length() - 2

    def one_pair(ds):
        r_raw = r_ref[:, ds]
        k_raw = k_ref[:, ds]
        v_raw = v_ref[:, ds]
        s_r = shift_lerp(r_raw, pr_ref[:, ds], mur_ref[:, ds])
        s_k = shift_lerp(k_raw, pk_ref[:, ds], muk_ref[:, ds])
        v = shift_lerp(v_raw, pv_ref[:, ds], muv_ref[:, ds])
        s0 = st_ref[:, ds]
        w0, a0, ka, rk = w0_ref[:, ds], a0_ref[:, ds], ka_ref[:, ds], rk_ref[:, ds]
        gng, gnb = gng_ref[:, ds], gnb_ref[:, ds]
        kkr = s_k * kk_ref[:, ds]
        lw = _mm(th_ref[...], wd_ref[:, ds], "nn", PL, PL)
        la = _mm(xa_ref[...], wa_ref[:, ds], "nn", PL, PL)
        gate = _mm(sg_ref[...], wg_ref[:, ds], "nn", PL, PL)
        n2 = _mm(kkr * kkr, ones_bd, "nn", SUM_PIECES, 1)
        bdv = _block_diag(v, m0)
        yield
        ld = -jnp.exp(-_softplus(-(w0 + lw)) - 0.5)
        a_sig = jax.nn.sigmoid(a0 + la)
        kk = kkr / jnp.maximum(jnp.sqrt(n2), L2_EPS)
        k2 = s_k * (1.0 + (a_sig - 1.0) * ka)
        bvec = kk * a_sig
        cs = _mm(tril, ld, "nn", 1, CUMSUM_PIECES)
        bonus = _mm(s_r * k2 * rk, ones_bd, "nn", SUM_PIECES, 1)
        yield
        cs_last = cs[C - 1:C, :]
        e_in = jnp.exp(-cs)
        e_end = jnp.exp(cs_last - cs)
        at = -kk * jnp.exp(cs - ld)
        rt = s_r * jnp.exp(cs)
        lhs = jnp.concatenate([at, rt], axis=0)
        rhs = jnp.concatenate(
            [_block_diag(k2 * e_in, m0), _block_diag(bvec * e_in, m0), _block_diag(s0, m0)], axis=0)
        akbs = _mm(lhs, rhs, "nt", P, P)
        ak, ab, ss0 = akbs[:, 0:PAIR], akbs[:, PAIR:2 * PAIR], akbs[:, 2 * PAIR:3 * PAIR]
        kb = jnp.concatenate([k2 * e_end, bvec * e_end], axis=0)
        yield
        a_ak = jnp.where(strict, ak[0:C], 0.0)
        a_rk = jnp.where(incl, ak[C:2 * C], 0.0)
        a_rb = jnp.where(incl, ab[C:2 * C], 0.0)
        m = jnp.where(strict, ab[0:C], 0.0)
        tinv = eye_p + m
        m = _mm(m, _block_diag(m, m0), "nn", P, P)
        akv = _mm(jnp.concatenate([a_ak, a_rk], axis=0), bdv, "nn", P, P)
        ru = ss0[0:C] + akv[0:C]
        y0 = ss0[C:2 * C] + akv[C:2 * C]
        yield
        for _ in range(n_dbl - 1):
            tm_ = _mm(jnp.concatenate([tinv, m], axis=0), _block_diag(m, m0), "nn", P, P)
            tinv = tinv + tm_[0:C]
            m = tm_[C:2 * C]
            yield
        tinv = tinv + _mm(tinv, _block_diag(m, m0), "nn", P, P)
        yield
        u = _mm(tinv, _block_diag(ru, m0), "nn", P, P)
        yield
        y = y0 + _mm(a_rb, _block_diag(u, m0), "nn", P, P)
        vu = jnp.concatenate([v, u], axis=0)
        dl = _mm(vu.T, kb, "nn", P, P)
        yield
        inv_n = 1.0 / HEAD
        mean = _mm(y, ones_bd, "nn", SUM_PIECES, 1) * inv_n
        s_new = s0 * jnp.exp(cs_last) + jnp.where(m0, dl[0:HEAD], dl[HEAD:PAIR])
        yield
        d = y - mean
        var = _mm(d * d, ones_bd, "nn", SUM_PIECES, 1) * inv_n
        yield
        yn = d * lax.rsqrt(var + RWKV_GN_EPS) * gng + gnb
        out = ((yn + bonus * v) * gate).astype(BF16)
        yield
        pr_ref[:, ds] = r_raw[C - 8:, :]
        pk_ref[:, ds] = k_raw[C - 8:, :]
        pv_ref[:, ds] = v_raw[C - 8:, :]
        st_ref[:, ds] = s_new
        o_ref[:, ds] = out

    def body(i, carry):
        gens = [one_pair(pl.ds(pl.multiple_of((i * UNR + k) * PAIR, PAIR), PAIR)) for k in range(UNR)]
        while gens:
            gens = [g for g in gens if next(g, "done") != "done"]
        return carry

    lax.fori_loop(0, NP // UNR, body, 0)


def _rwkv(u, B, L, D_R, rkv_blk, small_blk, params, P, PL, UNR):
    C = RWKV_CHUNK
    NC = L // C
    T = B * L
    NP = D_R // PAIR
    full = lambda a: pl.BlockSpec(a.shape, lambda b, c: (0,) * a.ndim)
    rowblk = lambda w, j: pl.BlockSpec((C, w), lambda b, c: (b * NC + c, j))
    return pl.pallas_call(
        functools.partial(_rwkv_kernel, NP=NP, UNR=UNR, P=P, PL=PL),
        out_shape=jax.ShapeDtypeStruct((T, D_R), BF16),
        grid=(B, NC),
        in_specs=[rowblk(D_R, rkv_blk), rowblk(D_R, rkv_blk + 1), rowblk(D_R, rkv_blk + 2),
                  rowblk(SMALL_W, small_blk)] + [full(a) for a in params],
        out_specs=pl.BlockSpec((C, D_R), lambda b, c: (b * NC + c, 0)),
        scratch_shapes=[
            pltpu.VMEM((HEAD, D_R), F32),
            pltpu.VMEM((8, D_R), F32), pltpu.VMEM((8, D_R), F32), pltpu.VMEM((8, D_R), F32),
            pltpu.VMEM((8, SMALL_W), F32),
            pltpu.VMEM((C, 128), F32), pltpu.VMEM((C, 256), F32), pltpu.VMEM((C, 384), F32),
        ],
        compiler_params=pltpu.CompilerParams(
            dimension_semantics=("arbitrary", "arbitrary"), vmem_limit_bytes=VMEM_LIMIT),
        name="rwkv",
    )(u, u, u, u, *params)


LN_ROWS = 64


def _layer_norm_rows(o_ref, g_ref, b_ref, ob_ref=None):
    tm = o_ref.shape[0]

    def body(i, carry):
        rs = pl.ds(pl.multiple_of(i * LN_ROWS, LN_ROWS), LN_ROWS)
        h = o_ref[rs, :]
        mu = jnp.mean(h, axis=-1, keepdims=True)
        d = h - mu
        var = jnp.mean(d * d, axis=-1, keepdims=True)
        y = d * lax.rsqrt(var + LN_EPS) * g_ref[...] + b_ref[...]
        o_ref[rs, :] = y
        if ob_ref is not None:
            ob_ref[rs, :] = y.astype(BF16)
        return carry

    lax.fori_loop(0, tm // LN_ROWS, body, 0)


_LN_PARAMS = pltpu.CompilerParams(dimension_semantics=("arbitrary", "arbitrary"), vmem_limit_bytes=VMEM_LIMIT)


def _outproj_ln_kernel(y1_ref, y2_ref, w1_ref, w2_ref, res_ref, g_ref, b_ref, o_ref, ob_ref, *, alpha, tn, nj):
    j = pl.program_id(1)
    acc = jnp.dot(y1_ref[...], w1_ref[...], preferred_element_type=F32)
    acc = acc + jnp.dot(y2_ref[...], w2_ref[...], preferred_element_type=F32)
    cs = pl.ds(pl.multiple_of(j * tn, tn), tn)
    o_ref[:, cs] = alpha * res_ref[...] + acc

    @pl.when(j == nj - 1)
    def _():
        _layer_norm_rows(o_ref, g_ref, b_ref, ob_ref)


def _outproj_ln(y1, y2, w1, w2, res, g, b, alpha, tm, tn):
    T, K1 = y1.shape
    K2 = y2.shape[1]
    D = w1.shape[1]
    nj = D // tn
    return pl.pallas_call(
        functools.partial(_outproj_ln_kernel, alpha=alpha, tn=tn, nj=nj),
        out_shape=(jax.ShapeDtypeStruct((T, D), F32), jax.ShapeDtypeStruct((T, D), BF16)),
        grid=(T // tm, nj),
        in_specs=[
            pl.BlockSpec((tm, K1), lambda i, j: (i, 0)),
            pl.BlockSpec((tm, K2), lambda i, j: (i, 0)),
            pl.BlockSpec((K1, tn), lambda i, j: (0, j)),
            pl.BlockSpec((K2, tn), lambda i, j: (0, j)),
            pl.BlockSpec((tm, tn), lambda i, j: (i, j)),
            pl.BlockSpec((1, D), lambda i, j: (0, 0)),
            pl.BlockSpec((1, D), lambda i, j: (0, 0)),
        ],
        out_specs=(pl.BlockSpec((tm, D), lambda i, j: (i, 0)), pl.BlockSpec((tm, D), lambda i, j: (i, 0))),
        compiler_params=_LN_PARAMS,
        name="outproj_ln",
    )(y1, y2, w1, w2, res, g, b)


def _ffn_ln_kernel(hb_ref, res_ref, wu_ref, wd_ref, g_ref, b_ref, o_ref, ob_ref, *, alpha, nf, tr):
    f = pl.program_id(1)

    @pl.when(f == 0)
    def _():
        o_ref[...] = jnp.zeros_like(o_ref)

    a = jnp.dot(hb_ref[...], wu_ref[...], preferred_element_type=F32)
    a = jnp.maximum(a, 0.0)
    a = (a * a).astype(BF16)
    o_ref[...] += jnp.dot(a, wd_ref[...], preferred_element_type=F32)

    cs = pl.ds(pl.multiple_of(f * tr, tr), tr)
    o_ref[:, cs] += alpha * res_ref[...]

    @pl.when(f == nf - 1)
    def _():
        _layer_norm_rows(o_ref, g_ref, b_ref, ob_ref)


def _ffn_ln(hb, h, wu, wd, g, b, alpha, tm, tf):
    T, D = h.shape
    F = wu.shape[1]
    nf = F // tf
    tr = D // nf
    assert tr % 128 == 0
    return pl.pallas_call(
        functools.partial(_ffn_ln_kernel, alpha=alpha, nf=nf, tr=tr),
        out_shape=(jax.ShapeDtypeStruct((T, D), F32), jax.ShapeDtypeStruct((T, D), BF16)),
        grid=(T // tm, nf),
        in_specs=[
            pl.BlockSpec((tm, D), lambda i, f: (i, 0)),
            pl.BlockSpec((tm, tr), lambda i, f: (i, f)),
            pl.BlockSpec((D, tf), lambda i, f: (0, f)),
            pl.BlockSpec((tf, D), lambda i, f: (f, 0)),
            pl.BlockSpec((1, D), lambda i, f: (0, 0)),
            pl.BlockSpec((1, D), lambda i, f: (0, 0)),
        ],
        out_specs=(pl.BlockSpec((tm, D), lambda i, f: (i, 0)), pl.BlockSpec((tm, D), lambda i, f: (i, 0))),
        compiler_params=_LN_PARAMS,
        name="ffn_ln",
    )(hb, h, wu, wd, g, b)


def _ple_ln_kernel(hb_ref, res_ref, p_ref, wg_ref, wp_ref, g_ref, b_ref, o_ref, *, alpha, tn, nj):
    j = pl.program_id(1)
    gate = jax.nn.sigmoid(jnp.dot(hb_ref[...], wg_ref[...], preferred_element_type=F32))
    emb = jnp.dot(p_ref[...].astype(BF16), wp_ref[...], preferred_element_type=F32)
    cs = pl.ds(pl.multiple_of(j * tn, tn), tn)
    o_ref[:, cs] = alpha * res_ref[...] + emb * gate

    @pl.when(j == nj - 1)
    def _():
        _layer_norm_rows(o_ref, g_ref, b_ref)


def _ple_ln(hb, h, p2d, wg, wp, g, b, alpha, tm, tn):
    T, D = h.shape
    DP = p2d.shape[1]
    nj = D // tn
    return pl.pallas_call(
        functools.partial(_ple_ln_kernel, alpha=alpha, tn=tn, nj=nj),
        out_shape=jax.ShapeDtypeStruct((T, D), F32),
        grid=(T // tm, nj),
        in_specs=[
            pl.BlockSpec((tm, D), lambda i, j: (i, 0)),
            pl.BlockSpec((tm, tn), lambda i, j: (i, j)),
            pl.BlockSpec((tm, DP), lambda i, j: (i, 0)),
            pl.BlockSpec((D, tn), lambda i, j: (0, j)),
            pl.BlockSpec((DP, tn), lambda i, j: (0, j)),
            pl.BlockSpec((1, D), lambda i, j: (0, 0)),
            pl.BlockSpec((1, D), lambda i, j: (0, 0)),
        ],
        out_specs=pl.BlockSpec((tm, D), lambda i, j: (i, 0)),
        compiler_params=_LN_PARAMS,
        name="ple_ln",
    )(hb, h, p2d, wg, wp, g, b)


def _pad_rows(w, rows, offset=0):
    out = jnp.zeros((rows, w.shape[1]), w.dtype)
    return out.at[offset:offset + w.shape[0]].set(w)


def _layer(h, p_i, w_in, conv_w, conv_b, dt_bias, A_log, D_skip, ssm_norm_g,
           rwkv_mu, w0, w_decay_b, a0, w_aaa_b, w_gate_b, k_k, k_a, r_k, gn_g, gn_b,
           w_out, ln1_g, ln1_b, w_up, w_down, ln2_g, ln2_b, w_ple, w_ple_gate, ln3_g, ln3_b, alpha):
    B, L, D = h.shape
    T = B * L
    D_SSM = ssm_norm_g.shape[0]
    H = dt_bias.shape[0]
    D_XBC = conv_w.shape[1]
    G = (D_XBC - D_SSM) // (2 * NSTATE)
    D_R = w0.shape[0]
    dl, al, gl = w_decay_b.shape[0], w_aaa_b.shape[0], w_gate_b.shape[0]
    dt_off = dl + al + gl
    assert D_XBC == 2 * D_SSM and D_R == D_SSM and D_SSM == H * HEAD and H == 4 * G
    assert dl <= 128 and dl + al <= 256 and 128 <= dl + al and dt_off + H <= SMALL_W
    assert dt_off // 128 == (dt_off + H - 1) // 128
    assert L % SSD_CHUNK == 0 and L % RWKV_CHUNK == 0 and D_R % PAIR == 0

    o_dt = D_SSM + D_XBC
    o_r = o_dt + H
    pad = SMALL_W - (dt_off + H)
    w_cat = _reorder_cols(w_in, [(0, o_dt), (o_r, 3 * D_R + dt_off), (o_dt, H), (None, pad)])
    small_blk = (D_SSM + D_XBC + 3 * D_R) // SMALL_W
    rkv_blk = (D_SSM + D_XBC) // D_R

    tm = min(512, T)
    n_cat = w_cat.shape[1]
    tn_in = next(t for t in (1280, 1024, 512) if n_cat % t == 0)
    tn = min(1024, D)
    u = _inproj(h.reshape(T, D), w_cat, tm, tn_in)

    row = lambda a: a.reshape(1, -1).astype(F32)
    lane_tile = lambda a: jnp.zeros((1, 128), F32).at[0, dt_off % 128:dt_off % 128 + H].set(a)
    hl = jnp.arange(128)[:, None] - dt_off % 128
    emat = (hl == (jnp.arange(D_SSM)[None, :] // HEAD)).astype(BF16)
    y_ssm = _ssd(u, B, L, D_SSM, G, H, dt_off, small_blk,
                 conv_w[:, :D_SSM], row(conv_b[:D_SSM]), conv_w[:, D_SSM:], row(conv_b[D_SSM:]),
                 lane_tile(dt_bias), lane_tile(A_log), row(jnp.repeat(D_skip, HEAD)), row(ssm_norm_g), emat, P=SSD_PIECES)

    mu = rwkv_mu
    mus = jnp.concatenate([mu[3 * D_R:], jnp.zeros((SMALL_W - dt_off,), F32)])
    params = [row(mu[:D_R]), row(mu[D_R:2 * D_R]), row(mu[2 * D_R:3 * D_R]), row(mus),
              row(w0), row(a0), row(k_k), row(k_a), row(r_k), row(gn_g), row(gn_b),
              _pad_rows(w_decay_b, 128), _pad_rows(w_aaa_b, 256, dl), _pad_rows(w_gate_b, 384, dl + al - 128)]
    y_rwkv = _rwkv(u, B, L, D_R, rkv_blk, small_blk, params, P=RWKV_PIECES, PL=LORA_PIECES,
                   UNR=min(RWKV_UNROLL, D_R // PAIR))

    wo = w_out.astype(BF16)
    h1, h1b = _outproj_ln(y_ssm, y_rwkv, wo[:D_SSM], wo[D_SSM:], h.reshape(T, D), row(ln1_g), row(ln1_b),
                          alpha, tm, tn)
    h2, h2b = _ffn_ln(h1b, h1, w_up.astype(BF16), w_down.astype(BF16), row(ln2_g), row(ln2_b), alpha, tm, 512)
    h3 = _ple_ln(h2b, h2, p_i.reshape(T, -1), w_ple_gate.astype(BF16), w_ple.astype(BF16),
                 row(ln3_g), row(ln3_b), alpha, tm, tn)
    return h3.reshape(B, L, D)


def kernel(x, p, w_in, conv_w, conv_b, dt_bias, A_log, D_skip, ssm_norm_g, rwkv_mu, w0, w_decay_b, a0, w_aaa_b, w_gate_b, k_k, k_a, r_k, gn_g, gn_b, w_out, ln1_g, ln1_b, w_up, w_down, ln2_g, ln2_b, w_ple, w_ple_gate, ln3_g, ln3_b):
    depth = w_in.shape[0]
    alpha = float((2 * depth) ** 0.25)
    h = x
    for i in range(depth):
        h = _layer(h, p[i], w_in[i], conv_w[i], conv_b[i], dt_bias[i], A_log[i], D_skip[i], ssm_norm_g[i],
                   rwkv_mu[i], w0[i], w_decay_b[i], a0[i], w_aaa_b[i], w_gate_b[i], k_k[i], k_a[i],
                   r_k[i].reshape(-1), gn_g[i], gn_b[i], w_out[i], ln1_g[i], ln1_b[i], w_up[i], w_down[i],
                   ln2_g[i], ln2_b[i], w_ple[i], w_ple_gate[i], ln3_g[i], ln3_b[i], alpha)
    return h
```

```python
import functools

import jax
import jax.numpy as jnp
from jax import lax
from jax.experimental import pallas as pl
from jax.experimental.pallas import tpu as pltpu

F32 = jnp.float32
BF16 = jnp.bfloat16

HEAD = 64
PAIR = 2 * HEAD
NSTATE = 128
SSD_CHUNK = 128
RWKV_CHUNK = 64
SMALL_W = 512
SSM_NORM_EPS = 1e-5
RWKV_GN_EPS = 64e-5
L2_EPS = 1e-12
LN_EPS = 1e-5
VMEM_LIMIT = 60 * 1024 * 1024
SSD_PIECES = 1
RWKV_PIECES = 1
LORA_PIECES = 1
SUM_PIECES = 1
CUMSUM_PIECES = 2
RWKV_UNROLL = 16

_DIMS = {
    "nn": (((1,), (0,)), ((), ())),
    "nt": (((1,), (1,)), ((), ())),
    "tn": (((0,), (0,)), ((), ())),
}


def _split(x, n):
    if x.dtype == BF16:
        return [x]
    parts = []
    rem = x
    for i in range(n):
        p = rem.astype(BF16)
        parts.append(p)
        if i + 1 < n:
            rem = rem - p.astype(F32)
    return parts


def _mm(a, b, dims="nn", pa=1, pb=1):
    aps = _split(a, pa)
    bps = _split(b, pb)
    lim = max(len(aps), len(bps))
    out = None
    for i in reversed(range(len(aps))):
        for j in reversed(range(len(bps))):
            if i + j < lim:
                t = lax.dot_general(aps[i], bps[j], _DIMS[dims], preferred_element_type=F32)
                out = t if out is None else out + t
    return out


def _softplus(x):
    return jnp.maximum(x, 0.0) + jnp.log(1.0 + jnp.exp(-jnp.abs(x)))


def _shift_rows(cur, prev8, j):
    rc = pltpu.roll(cur, j, 0)
    rp = pltpu.roll(prev8, j, 0)
    row = lax.broadcasted_iota(jnp.int32, rp.shape, 0)
    first = jnp.where(row < j, rp, rc[0:8, :])
    return jnp.concatenate([first, rc[8:, :]], axis=0)


def _block_diag(x, m0):
    return jnp.concatenate([jnp.where(m0, x, 0.0), jnp.where(m0, 0.0, x)], axis=0)


PREP_COLS = 256


def _reorder_rows_kernel(w_ref, o_ref, *, pieces):
    pos = 0
    for s, n in pieces:
        if s is None:
            o_ref[pos:pos + n, :] = jnp.zeros((n, o_ref.shape[1]), BF16)
        else:
            o_ref[pos:pos + n, :] = w_ref[s:s + n, :].astype(BF16)
        pos += n


def _reorder_rows(wt, pieces):
    n_src, K = wt.shape
    n_out = sum(n for _, n in pieces)
    pos = 0
    for s, n in pieces:
        assert s is None or (pos % 16 == 0 and s % 8 == 0)
        pos += n
    kb = min(PREP_COLS, K)
    return pl.pallas_call(
        functools.partial(_reorder_rows_kernel, pieces=tuple(pieces)),
        out_shape=jax.ShapeDtypeStruct((n_out, K), BF16),
        grid=(K // kb,),
        in_specs=[pl.BlockSpec((n_src, kb), lambda i: (0, i))],
        out_specs=pl.BlockSpec((n_out, kb), lambda i: (0, i)),
        compiler_params=pltpu.CompilerParams(dimension_semantics=("arbitrary",), vmem_limit_bytes=VMEM_LIMIT),
        name="reorder_rows",
    )(wt)


def _inproj_kernel(x_ref, wt_ref, o_ref, xb_ref):
    @pl.when(pl.program_id(1) == 0)
    def _():
        xb_ref[...] = x_ref[...].astype(BF16)

    o_ref[...] = lax.dot_general(xb_ref[...], wt_ref[...], _DIMS["nt"], preferred_element_type=F32)


def _inproj(x2d, wt_bf16, tm, tn):
    T, D = x2d.shape
    N = wt_bf16.shape[0]
    return pl.pallas_call(
        _inproj_kernel,
        out_shape=jax.ShapeDtypeStruct((T, N), F32),
        grid=(T // tm, N // tn),
        in_specs=[
            pl.BlockSpec((tm, D), lambda i, j: (i, 0)),
            pl.BlockSpec((tn, D), lambda i, j: (j, 0)),
        ],
        out_specs=pl.BlockSpec((tm, tn), lambda i, j: (i, j)),
        scratch_shapes=[pltpu.VMEM((tm, D), BF16)],
        compiler_params=pltpu.CompilerParams(
            dimension_semantics=("arbitrary", "arbitrary"), vmem_limit_bytes=VMEM_LIMIT),
        name="inproj",
    )(x2d, wt_bf16)


def _conv_silu(cur_ref, prev_ref, w_ref, b_ref):
    cur = cur_ref[...]
    prev8 = prev_ref[...]
    q = cur.shape[0]
    acc = b_ref[...] + w_ref[3:4, :] * cur
    for j in (1, 2, 3):
        acc = acc + w_ref[3 - j:4 - j, :] * _shift_rows(cur, prev8, j)
    prev_ref[...] = cur[q - 8:, :]
    return acc * jax.nn.sigmoid(acc)


def _ssd_kernel(z_ref, xs_ref, bc_ref, sm_ref, cwx_ref, cbx_ref, cwbc_ref, cbbc_ref,
                dtb_ref, alog_ref, dsk_ref, ng_ref, e_ref, o_ref,
                st_ref, px_ref, pbc_ref, xc_ref, bcc_ref, ex_ref, *, G, H, dt_off, P):
    Q = xs_ref.shape[0]
    N = NSTATE
    GW = 4 * HEAD

    @pl.when(pl.program_id(1) == 0)
    def _():
        st_ref[...] = jnp.zeros_like(st_ref)
        px_ref[...] = jnp.zeros_like(px_ref)
        pbc_ref[...] = jnp.zeros_like(pbc_ref)

    xc_ref[...] = _conv_silu(xs_ref, px_ref, cwx_ref, cbx_ref)
    bcc_ref[...] = _conv_silu(bc_ref, pbc_ref, cwbc_ref, cbbc_ref)

    tile = dt_off // 128
    dt_lane = dt_off % 128
    lane = lax.broadcasted_iota(jnp.int32, (1, 128), 1)
    hmask = (lane >= dt_lane) & (lane < dt_lane + H)
    raw = sm_ref[:, tile * 128:(tile + 1) * 128]
    dt = jnp.where(hmask, _softplus(raw + dtb_ref[...]), 0.0)
    a_neg = jnp.where(hmask, -jnp.exp(alog_ref[...]), 0.0)
    a = dt * a_neg
    row = lax.broadcasted_iota(jnp.int32, (Q, Q), 0)
    col = lax.broadcasted_iota(jnp.int32, (Q, Q), 1)
    causal = col <= row
    tril = jnp.where(causal, 1.0, 0.0).astype(BF16)
    a_cs = _mm(tril, a, "nn", 1, 3)
    a_last = a_cs[Q - 1:Q, :]
    a_cs_t = a_cs.T
    stack = jnp.concatenate(
        [dt, jnp.exp(a_cs), jnp.exp(a_last - a_cs), jnp.broadcast_to(jnp.exp(a_last), (8, 128))], axis=0)
    ex_ref[...] = _mm(stack, e_ref[...], "nn", 3, 1)

    lane_p = lax.broadcasted_iota(jnp.int32, (1, PAIR), 1)
    m0 = lane_p < HEAD
    for g in range(G):
        gs = slice(g * GW, (g + 1) * GW)
        bg = bcc_ref[:, g * N:(g + 1) * N]
        cg = bcc_ref[:, G * N + g * N:G * N + (g + 1) * N]
        scores = _mm(cg, bg, "nt", P, P)
        xsg = xc_ref[:, gs]
        xdt = xsg * ex_ref[0:Q, gs]
        st = st_ref[g]
        y = _mm(cg, st, "nn", P, P) * ex_ref[Q:2 * Q, gs]
        ypairs = []
        for pr in range(2):
            xp = xdt[:, pr * PAIR:(pr + 1) * PAIR]
            acc = None
            for half in range(2):
                hl = dt_lane + g * 4 + pr * 2 + half
                seg = a_cs[:, hl:hl + 1] - a_cs_t[hl:hl + 1, :]
                lm = jnp.exp(jnp.where(causal, seg, -jnp.inf))
                xm = jnp.where(m0, xp, 0.0) if half == 0 else jnp.where(m0, 0.0, xp)
                t = _mm(scores * lm, xm, "nn", P, P)
                acc = t if acc is None else acc + t
            ypairs.append(acc)
        y = y + jnp.concatenate(ypairs, axis=1)
        st_ref[g] = st * ex_ref[3 * Q:3 * Q + 1, gs] + _mm(bg.T, xdt * ex_ref[2 * Q:3 * Q, gs], "nn", P, P)
        y = y + dsk_ref[:, gs] * xsg
        zg = z_ref[:, gs]
        v = y * (zg * jax.nn.sigmoid(zg))
        ms = jnp.mean(v * v, axis=-1, keepdims=True)
        o_ref[:, gs] = (v * lax.rsqrt(ms + SSM_NORM_EPS) * ng_ref[:, gs]).astype(BF16)


def _ssd(u, B, L, D_SSM, G, H, dt_off, small_blk, cwx, cbx, cwbc, cbbc, dtb, alog, dsk, ng, emat, P):
    Q = SSD_CHUNK
    NC = L // Q
    T = B * L
    full = lambda shape: pl.BlockSpec(shape, lambda b, c: (0,) * len(shape))
    rowblk = lambda w, j: pl.BlockSpec((Q, w), lambda b, c: (b * NC + c, j))
    return pl.pallas_call(
        functools.partial(_ssd_kernel, G=G, H=H, dt_off=dt_off, P=P),
        out_shape=jax.ShapeDtypeStruct((T, D_SSM), BF16),
        grid=(B, NC),
        in_specs=[
            rowblk(D_SSM, 0), rowblk(D_SSM, 1), rowblk(D_SSM, 2), rowblk(SMALL_W, small_blk),
            full(cwx.shape), full(cbx.shape), full(cwbc.shape), full(cbbc.shape),
            full(dtb.shape), full(alog.shape), full(dsk.shape), full(ng.shape), full(emat.shape),
        ],
        out_specs=pl.BlockSpec((Q, D_SSM), lambda b, c: (b * NC + c, 0)),
        scratch_shapes=[
            pltpu.VMEM((G, NSTATE, 4 * HEAD), F32),
            pltpu.VMEM((8, D_SSM), F32),
            pltpu.VMEM((8, D_SSM), F32),
            pltpu.VMEM((Q, D_SSM), F32),
            pltpu.VMEM((Q, D_SSM), F32),
            pltpu.VMEM((3 * Q + 8, D_SSM), F32),
        ],
        compiler_params=pltpu.CompilerParams(
            dimension_semantics=("arbitrary", "arbitrary"), vmem_limit_bytes=VMEM_LIMIT),
        name="ssd",
    )(u, u, u, u, cwx, cbx, cwbc, cbbc, dtb, alog, dsk, ng, emat)


def _rwkv_kernel(r_ref, k_ref, v_ref, sm_ref, mur_ref, muk_ref, muv_ref, mus_ref,
                 w0_ref, a0_ref, kk_ref, ka_ref, rk_ref, gng_ref, gnb_ref,
                 wd_ref, wa_ref, wg_ref, o_ref,
                 st_ref, pr_ref, pk_ref, pv_ref, ps_ref, th_ref, xa_ref, sg_ref, *, NP, UNR, P, PL):
    C = r_ref.shape[0]

    @pl.when(pl.program_id(1) == 0)
    def _():
        st_ref[...] = jnp.zeros_like(st_ref)
        pr_ref[...] = jnp.zeros_like(pr_ref)
        pk_ref[...] = jnp.zeros_like(pk_ref)
        pv_ref[...] = jnp.zeros_like(pv_ref)
        ps_ref[...] = jnp.zeros_like(ps_ref)

    def shift_lerp(cur, prev8, mu):
        prev = _shift_rows(cur, prev8, 1)
        return cur + (prev - cur) * mu

    sm = sm_ref[...]
    ss = shift_lerp(sm, ps_ref[...], mus_ref[...])
    ps_ref[...] = sm[C - 8:, :]
    th_ref[...] = jnp.tanh(ss[:, 0:128])
    xa_ref[...] = ss[:, 0:256]
    sg_ref[...] = jax.nn.sigmoid(ss[:, 128:512])

    lane = lax.broadcasted_iota(jnp.int32, (C, PAIR), 1)
    rowi = lax.broadcasted_iota(jnp.int32, (C, PAIR), 0)
    m0 = lane < HEAD
    s_idx = jnp.where(m0, lane, lane - HEAD)
    strict = s_idx < rowi
    incl = s_idx <= rowi
    eye_p = jnp.where(s_idx == rowi, 1.0, 0.0)
    r2 = lax.broadcasted_iota(jnp.int32, (PAIR, PAIR), 0)
    c2 = lax.broadcasted_iota(jnp.int32, (PAIR, PAIR), 1)
    ones_bd = jnp.where((r2 < HEAD) == (c2 < HEAD), 1.0, 0.0).astype(BF16)
    r3 = lax.broadcasted_iota(jnp.int32, (C, C), 0)
    c3 = lax.broadcasted_iota(jnp.int32, (C, C), 1)
    tril = jnp.where(c3 <= r3, 1.0, 0.0).astype(BF16)
    n_dbl = C.bit_length() - 2

    def one_pair(ds):
        r_raw = r_ref[:, ds]
        k_raw = k_ref[:, ds]
        v_raw = v_ref[:, ds]
        s_r = shift_lerp(r_raw, pr_ref[:, ds], mur_ref[:, ds])
        s_k = shift_lerp(k_raw, pk_ref[:, ds], muk_ref[:, ds])
        v = shift_lerp(v_raw, pv_ref[:, ds], muv_ref[:, ds])
        s0 = st_ref[:, ds]
        w0, a0, ka, rk = w0_ref[:, ds], a0_ref[:, ds], ka_ref[:, ds], rk_ref[:, ds]
        gng, gnb = gng_ref[:, ds], gnb_ref[:, ds]
        kkr = s_k * kk_ref[:, ds]
        lw = _mm(th_ref[...], wd_ref[:, ds], "nn", PL, PL)
        la = _mm(xa_ref[...], wa_ref[:, ds], "nn", PL, PL)
        gate = _mm(sg_ref[...], wg_ref[:, ds], "nn", PL, PL)
        n2 = _mm(kkr * kkr, ones_bd, "nn", SUM_PIECES, 1)
        bdv = _block_diag(v, m0)
        yield
        ld = -jnp.exp(-_softplus(-(w0 + lw)) - 0.5)
        a_sig = jax.nn.sigmoid(a0 + la)
        kk = kkr / jnp.maximum(jnp.sqrt(n2), L2_EPS)
        k2 = s_k * (1.0 + (a_sig - 1.0) * ka)
        bvec = kk * a_sig
        cs = _mm(tril, ld, "nn", 1, CUMSUM_PIECES)
        bonus = _mm(s_r * k2 * rk, ones_bd, "nn", SUM_PIECES, 1)
        yield
        cs_last = cs[C - 1:C, :]
        e_in = jnp.exp(-cs)
        e_end = jnp.exp(cs_last - cs)
        at = -kk * jnp.exp(cs - ld)
        rt = s_r * jnp.exp(cs)
        lhs = jnp.concatenate([at, rt], axis=0)
        rhs = jnp.concatenate(
            [_block_diag(k2 * e_in, m0), _block_diag(bvec * e_in, m0), _block_diag(s0, m0)], axis=0)
        akbs = _mm(lhs, rhs, "nt", P, P)
        ak, ab, ss0 = akbs[:, 0:PAIR], akbs[:, PAIR:2 * PAIR], akbs[:, 2 * PAIR:3 * PAIR]
        kb = jnp.concatenate([k2 * e_end, bvec * e_end], axis=0)
        yield
        a_ak = jnp.where(strict, ak[0:C], 0.0)
        a_rk = jnp.where(incl, ak[C:2 * C], 0.0)
        a_rb = jnp.where(incl, ab[C:2 * C], 0.0)
        m = jnp.where(strict, ab[0:C], 0.0)
        tinv = eye_p + m
        m = _mm(m, _block_diag(m, m0), "nn", P, P)
        akv = _mm(jnp.concatenate([a_ak, a_rk], axis=0), bdv, "nn", P, P)
        ru = ss0[0:C] + akv[0:C]
        y0 = ss0[C:2 * C] + akv[C:2 * C]
        yield
        for _ in range(n_dbl - 1):
            tm_ = _mm(jnp.concatenate([tinv, m], axis=0), _block_diag(m, m0), "nn", P, P)
            tinv = tinv + tm_[0:C]
            m = tm_[C:2 * C]
            yield
        tinv = tinv + _mm(tinv, _block_diag(m, m0), "nn", P, P)
        yield
        u = _mm(tinv, _block_diag(ru, m0), "nn", P, P)
        yield
        y = y0 + _mm(a_rb, _block_diag(u, m0), "nn", P, P)
        vu = jnp.concatenate([v, u], axis=0)
        dl = _mm(vu.T, kb, "nn", P, P)
        yield
        inv_n = 1.0 / HEAD
        mean = _mm(y, ones_bd, "nn", SUM_PIECES, 1) * inv_n
        s_new = s0 * jnp.exp(cs_last) + jnp.where(m0, dl[0:HEAD], dl[HEAD:PAIR])
        yield
        d = y - mean
        var = _mm(d * d, ones_bd, "nn", SUM_PIECES, 1) * inv_n
        yield
        yn = d * lax.rsqrt(var + RWKV_GN_EPS) * gng + gnb
        out = ((yn + bonus * v) * gate).astype(BF16)
        yield
        pr_ref[:, ds] = r_raw[C - 8:, :]
        pk_ref[:, ds] = k_raw[C - 8:, :]
        pv_ref[:, ds] = v_raw[C - 8:, :]
        st_ref[:, ds] = s_new
        o_ref[:, ds] = out

    def body(i, carry):
        gens = [one_pair(pl.ds(pl.multiple_of((i * UNR + k) * PAIR, PAIR), PAIR)) for k in range(UNR)]
        while gens:
            gens = [g for g in gens if next(g, "done") != "done"]
        return carry

    lax.fori_loop(0, NP // UNR, body, 0)


def _rwkv(u, B, L, D_R, rkv_blk, small_blk, params, P, PL, UNR):
    C = RWKV_CHUNK
    NC = L // C
    T = B * L
    NP = D_R // PAIR
    full = lambda a: pl.BlockSpec(a.shape, lambda b, c: (0,) * a.ndim)
    rowblk = lambda w, j: pl.BlockSpec((C, w), lambda b, c: (b * NC + c, j))
    return pl.pallas_call(
        functools.partial(_rwkv_kernel, NP=NP, UNR=UNR, P=P, PL=PL),
        out_shape=jax.ShapeDtypeStruct((T, D_R), BF16),
        grid=(B, NC),
        in_specs=[rowblk(D_R, rkv_blk), rowblk(D_R, rkv_blk + 1), rowblk(D_R, rkv_blk + 2),
                  rowblk(SMALL_W, small_blk)] + [full(a) for a in params],
        out_specs=pl.BlockSpec((C, D_R), lambda b, c: (b * NC + c, 0)),
        scratch_shapes=[
            pltpu.VMEM((HEAD, D_R), F32),
            pltpu.VMEM((8, D_R), F32), pltpu.VMEM((8, D_R), F32), pltpu.VMEM((8, D_R), F32),
            pltpu.VMEM((8, SMALL_W), F32),
            pltpu.VMEM((C, 128), F32), pltpu.VMEM((C, 256), F32), pltpu.VMEM((C, 384), F32),
        ],
        compiler_params=pltpu.CompilerParams(
            dimension_semantics=("arbitrary", "arbitrary"), vmem_limit_bytes=VMEM_LIMIT),
        name="rwkv",
    )(u, u, u, u, *params)


LN_ROWS = 128


def _layer_norm_rows(o_ref, g_ref, b_ref, ob_ref=None):
    tm = o_ref.shape[0]

    def body(i, carry):
        rs = pl.ds(pl.multiple_of(i * LN_ROWS, LN_ROWS), LN_ROWS)
        h = o_ref[rs, :]
        mu = jnp.mean(h, axis=-1, keepdims=True)
        d = h - mu
        var = jnp.mean(d * d, axis=-1, keepdims=True)
        y = d * lax.rsqrt(var + LN_EPS) * g_ref[...] + b_ref[...]
        o_ref[rs, :] = y
        if ob_ref is not None:
            ob_ref[rs, :] = y.astype(BF16)
        return carry

    lax.fori_loop(0, tm // LN_ROWS, body, 0)


_LN_PARAMS = pltpu.CompilerParams(dimension_semantics=("arbitrary", "arbitrary"), vmem_limit_bytes=VMEM_LIMIT)


def _outproj_ln_kernel(y1_ref, y2_ref, w1_ref, w2_ref, res_ref, g_ref, b_ref, o_ref, ob_ref, *, alpha, tn, nj):
    j = pl.program_id(1)
    acc = jnp.dot(y1_ref[...], w1_ref[...], preferred_element_type=F32)
    acc = acc + jnp.dot(y2_ref[...], w2_ref[...], preferred_element_type=F32)
    cs = pl.ds(pl.multiple_of(j * tn, tn), tn)
    o_ref[:, cs] = alpha * res_ref[...] + acc

    @pl.when(j == nj - 1)
    def _():
        _layer_norm_rows(o_ref, g_ref, b_ref, ob_ref)


def _outproj_ln(y1, y2, w1, w2, res, g, b, alpha, tm, tn):
    T, K1 = y1.shape
    K2 = y2.shape[1]
    D = w1.shape[1]
    nj = D // tn
    return pl.pallas_call(
        functools.partial(_outproj_ln_kernel, alpha=alpha, tn=tn, nj=nj),
        out_shape=(jax.ShapeDtypeStruct((T, D), F32), jax.ShapeDtypeStruct((T, D), BF16)),
        grid=(T // tm, nj),
        in_specs=[
            pl.BlockSpec((tm, K1), lambda i, j: (i, 0)),
            pl.BlockSpec((tm, K2), lambda i, j: (i, 0)),
            pl.BlockSpec((K1, tn), lambda i, j: (0, j)),
            pl.BlockSpec((K2, tn), lambda i, j: (0, j)),
            pl.BlockSpec((tm, tn), lambda i, j: (i, j)),
            pl.BlockSpec((1, D), lambda i, j: (0, 0)),
            pl.BlockSpec((1, D), lambda i, j: (0, 0)),
        ],
        out_specs=(pl.BlockSpec((tm, D), lambda i, j: (i, 0)), pl.BlockSpec((tm, D), lambda i, j: (i, 0))),
        compiler_params=_LN_PARAMS,
        name="outproj_ln",
    )(y1, y2, w1, w2, res, g, b)


def _ffn_ln_kernel(hb_ref, res_ref, wu_ref, wd_ref, g_ref, b_ref, o_ref, ob_ref, *, alpha, nf, tr):
    f = pl.program_id(1)

    @pl.when(f == 0)
    def _():
        o_ref[...] = jnp.zeros_like(o_ref)

    a = jnp.dot(hb_ref[...], wu_ref[...], preferred_element_type=F32)
    a = jnp.maximum(a, 0.0)
    a = (a * a).astype(BF16)
    o_ref[...] += jnp.dot(a, wd_ref[...], preferred_element_type=F32)

    cs = pl.ds(pl.multiple_of(f * tr, tr), tr)
    o_ref[:, cs] += alpha * res_ref[...]

    @pl.when(f == nf - 1)
    def _():
        _layer_norm_rows(o_ref, g_ref, b_ref, ob_ref)


def _ffn_ln(hb, h, wu, wd, g, b, alpha, tm, tf):
    T, D = h.shape
    F = wu.shape[1]
    nf = F // tf
    tr = D // nf
    assert tr % 128 == 0
    return pl.pallas_call(
        functools.partial(_ffn_ln_kernel, alpha=alpha, nf=nf, tr=tr),
        out_shape=(jax.ShapeDtypeStruct((T, D), F32), jax.ShapeDtypeStruct((T, D), BF16)),
        grid=(T // tm, nf),
        in_specs=[
            pl.BlockSpec((tm, D), lambda i, f: (i, 0)),
            pl.BlockSpec((tm, tr), lambda i, f: (i, f)),
            pl.BlockSpec((D, tf), lambda i, f: (0, f)),
            pl.BlockSpec((tf, D), lambda i, f: (f, 0)),
            pl.BlockSpec((1, D), lambda i, f: (0, 0)),
            pl.BlockSpec((1, D), lambda i, f: (0, 0)),
        ],
        out_specs=(pl.BlockSpec((tm, D), lambda i, f: (i, 0)), pl.BlockSpec((tm, D), lambda i, f: (i, 0))),
        compiler_params=_LN_PARAMS,
        name="ffn_ln",
    )(hb, h, wu, wd, g, b)


def _ple_ln_kernel(hb_ref, res_ref, p_ref, wg_ref, wp_ref, g_ref, b_ref, o_ref, *, alpha, tn, nj):
    j = pl.program_id(1)
    gate = jax.nn.sigmoid(jnp.dot(hb_ref[...], wg_ref[...], preferred_element_type=F32))
    emb = jnp.dot(p_ref[...].astype(BF16), wp_ref[...], preferred_element_type=F32)
    cs = pl.ds(pl.multiple_of(j * tn, tn), tn)
    o_ref[:, cs] = alpha * res_ref[...] + emb * gate

    @pl.when(j == nj - 1)
    def _():
        _layer_norm_rows(o_ref, g_ref, b_ref)


def _ple_ln(hb, h, p2d, wg, wp, g, b, alpha, tm, tn):
    T, D = h.shape
    DP = p2d.shape[1]
    nj = D // tn
    return pl.pallas_call(
        functools.partial(_ple_ln_kernel, alpha=alpha, tn=tn, nj=nj),
        out_shape=jax.ShapeDtypeStruct((T, D), F32),
        grid=(T // tm, nj),
        in_specs=[
            pl.BlockSpec((tm, D), lambda i, j: (i, 0)),
            pl.BlockSpec((tm, tn), lambda i, j: (i, j)),
            pl.BlockSpec((tm, DP), lambda i, j: (i, 0)),
            pl.BlockSpec((D, tn), lambda i, j: (0, j)),
            pl.BlockSpec((DP, tn), lambda i, j: (0, j)),
            pl.BlockSpec((1, D), lambda i, j: (0, 0)),
            pl.BlockSpec((1, D), lambda i, j: (0, 0)),
        ],
        out_specs=pl.BlockSpec((tm, D), lambda i, j: (i, 0)),
        compiler_params=_LN_PARAMS,
        name="ple_ln",
    )(hb, h, p2d, wg, wp, g, b)


def _pad_rows(w, rows, offset=0):
    out = jnp.zeros((rows, w.shape[1]), w.dtype)
    return out.at[offset:offset + w.shape[0]].set(w)


def _layer(h, p_i, w_in, conv_w, conv_b, dt_bias, A_log, D_skip, ssm_norm_g,
           rwkv_mu, w0, w_decay_b, a0, w_aaa_b, w_gate_b, k_k, k_a, r_k, gn_g, gn_b,
           w_out, ln1_g, ln1_b, w_up, w_down, ln2_g, ln2_b, w_ple, w_ple_gate, ln3_g, ln3_b, alpha):
    B, L, D = h.shape
    T = B * L
    D_SSM = ssm_norm_g.shape[0]
    H = dt_bias.shape[0]
    D_XBC = conv_w.shape[1]
    G = (D_XBC - D_SSM) // (2 * NSTATE)
    D_R = w0.shape[0]
    dl, al, gl = w_decay_b.shape[0], w_aaa_b.shape[0], w_gate_b.shape[0]
    dt_off = dl + al + gl
    assert D_XBC == 2 * D_SSM and D_R == D_SSM and D_SSM == H * HEAD and H == 4 * G
    assert dl <= 128 and dl + al <= 256 and 128 <= dl + al and dt_off + H <= SMALL_W
    assert dt_off // 128 == (dt_off + H - 1) // 128
    assert L % SSD_CHUNK == 0 and L % RWKV_CHUNK == 0 and D_R % PAIR == 0

    o_dt = D_SSM + D_XBC
    o_r = o_dt + H
    pad = SMALL_W - (dt_off + H)
    wt_cat = _reorder_rows(w_in.T, [(0, o_dt), (o_r, 3 * D_R + dt_off), (o_dt, H), (None, pad)])
    small_blk = (D_SSM + D_XBC + 3 * D_R) // SMALL_W
    rkv_blk = (D_SSM + D_XBC) // D_R

    tm = min(512, T)
    n_cat = wt_cat.shape[0]
    tn_in = next(t for t in (1280, 1024, 512) if n_cat % t == 0)
    tn = min(1024, D)
    u = _inproj(h.reshape(T, D), wt_cat, tm, tn_in)

    row = lambda a: a.reshape(1, -1).astype(F32)
    lane_tile = lambda a: jnp.zeros((1, 128), F32).at[0, dt_off % 128:dt_off % 128 + H].set(a)
    hl = jnp.arange(128)[:, None] - dt_off % 128
    emat = (hl == (jnp.arange(D_SSM)[None, :] // HEAD)).astype(BF16)
    y_ssm = _ssd(u, B, L, D_SSM, G, H, dt_off, small_blk,
                 conv_w[:, :D_SSM], row(conv_b[:D_SSM]), conv_w[:, D_SSM:], row(conv_b[D_SSM:]),
                 lane_tile(dt_bias), lane_tile(A_log), row(jnp.repeat(D_skip, HEAD)), row(ssm_norm_g), emat, P=SSD_PIECES)

    mu = rwkv_mu
    mus = jnp.concatenate([mu[3 * D_R:], jnp.zeros((SMALL_W - dt_off,), F32)])
    params = [row(mu[:D_R]), row(mu[D_R:2 * D_R]), row(mu[2 * D_R:3 * D_R]), row(mus),
              row(w0), row(a0), row(k_k), row(k_a), row(r_k), row(gn_g), row(gn_b),
              _pad_rows(w_decay_b, 128), _pad_rows(w_aaa_b, 256, dl), _pad_rows(w_gate_b, 384, dl + al - 128)]
    y_rwkv = _rwkv(u, B, L, D_R, rkv_blk, small_blk, params, P=RWKV_PIECES, PL=LORA_PIECES,
                   UNR=min(RWKV_UNROLL, D_R // PAIR))

    wo = w_out.astype(BF16)
    h1, h1b = _outproj_ln(y_ssm, y_rwkv, wo[:D_SSM], wo[D_SSM:], h.reshape(T, D), row(ln1_g), row(ln1_b),
                          alpha, tm, tn)
    h2, h2b = _ffn_ln(h1b, h1, w_up.astype(BF16), w_down.astype(BF16), row(ln2_g), row(ln2_b), alpha, tm, 512)
    h3 = _ple_ln(h2b, h2, p_i.reshape(T, -1), w_ple_gate.astype(BF16), w_ple.astype(BF16),
                 row(ln3_g), row(ln3_b), alpha, tm, tn)
    return h3.reshape(B, L, D)


def kernel(x, p, w_in, conv_w, conv_b, dt_bias, A_log, D_skip, ssm_norm_g, rwkv_mu, w0, w_decay_b, a0, w_aaa_b, w_gate_b, k_k, k_a, r_k, gn_g, gn_b, w_out, ln1_g, ln1_b, w_up, w_down, ln2_g, ln2_b, w_ple, w_ple_gate, ln3_g, ln3_b):
    depth = w_in.shape[0]
    alpha = float((2 * depth) ** 0.25)
    h = x
    for i in range(depth):
        h = _layer(h, p[i], w_in[i], conv_w[i], conv_b[i], dt_bias[i], A_log[i], D_skip[i], ssm_norm_g[i],
                   rwkv_mu[i], w0[i], w_decay_b[i], a0[i], w_aaa_b[i], w_gate_b[i], k_k[i], k_a[i],
                   r_k[i].reshape(-1), gn_g[i], gn_b[i], w_out[i], ln1_g[i], ln1_b[i], w_up[i], w_down[i],
                   ln2_g[i], ln2_b[i], w_ple[i], w_ple_gate[i], ln3_g[i], ln3_b[i], alpha)
    return h
```

```python
import functools

import jax
import jax.numpy as jnp
from jax import lax
from jax.experimental import pallas as pl
from jax.experimental.pallas import tpu as pltpu

F32 = jnp.float32
BF16 = jnp.bfloat16

HEAD = 64
PAIR = 2 * HEAD
NSTATE = 128
SSD_CHUNK = 128
RWKV_CHUNK = 64
SMALL_W = 512
SSM_NORM_EPS = 1e-5
RWKV_GN_EPS = 64e-5
L2_EPS = 1e-12
LN_EPS = 1e-5
VMEM_LIMIT = 60 * 1024 * 1024
SSD_PIECES = 1
RWKV_PIECES = 1
LORA_PIECES = 1
SUM_PIECES = 1
CUMSUM_PIECES = 2
RWKV_UNROLL = 16

_DIMS = {
    "nn": (((1,), (0,)), ((), ())),
    "nt": (((1,), (1,)), ((), ())),
    "tn": (((0,), (0,)), ((), ())),
}


def _split(x, n):
    if x.dtype == BF16:
        return [x]
    parts = []
    rem = x
    for i in range(n):
        p = rem.astype(BF16)
        parts.append(p)
        if i + 1 < n:
            rem = rem - p.astype(F32)
    return parts


def _mm(a, b, dims="nn", pa=1, pb=1):
    aps = _split(a, pa)
    bps = _split(b, pb)
    lim = max(len(aps), len(bps))
    out = None
    for i in reversed(range(len(aps))):
        for j in reversed(range(len(bps))):
            if i + j < lim:
                t = lax.dot_general(aps[i], bps[j], _DIMS[dims], preferred_element_type=F32)
                out = t if out is None else out + t
    return out


def _softplus(x):
    return jnp.maximum(x, 0.0) + jnp.log(1.0 + jnp.exp(-jnp.abs(x)))


def _shift_rows(cur, prev8, j):
    rc = pltpu.roll(cur, j, 0)
    rp = pltpu.roll(prev8, j, 0)
    row = lax.broadcasted_iota(jnp.int32, rp.shape, 0)
    first = jnp.where(row < j, rp, rc[0:8, :])
    return jnp.concatenate([first, rc[8:, :]], axis=0)


def _block_diag(x, m0):
    return jnp.concatenate([jnp.where(m0, x, 0.0), jnp.where(m0, 0.0, x)], axis=0)


def _with_cast_riders(body, n_in, n_out, n_riders):
    def wrapped(*refs):
        ins, rest = refs[:n_in], refs[n_in:]
        rin, rest = rest[:n_riders], rest[n_riders:]
        outs, rest = rest[:n_out], rest[n_out:]
        rout, scratch = rest[:n_riders], rest[n_riders:]
        for src, dst in zip(rin, rout):
            dst[...] = src[...].astype(BF16)
        body(*ins, *outs, *scratch)
    return wrapped


def _rider_specs(riders, n_steps, index_map):
    specs, shapes = [], []
    for w in riders:
        rows = w.shape[0] // n_steps
        assert rows * n_steps == w.shape[0] and rows % 16 == 0
        specs.append(pl.BlockSpec((rows, w.shape[1]), index_map))
        shapes.append(jax.ShapeDtypeStruct(w.shape, BF16))
    return specs, shapes


PREP_COLS = 256


def _reorder_rows_kernel(w_ref, o_ref, *, pieces):
    pos = 0
    for s, n in pieces:
        if s is None:
            o_ref[pos:pos + n, :] = jnp.zeros((n, o_ref.shape[1]), BF16)
        else:
            o_ref[pos:pos + n, :] = w_ref[s:s + n, :].astype(BF16)
        pos += n


def _reorder_rows(wt, pieces):
    n_src, K = wt.shape
    n_out = sum(n for _, n in pieces)
    pos = 0
    for s, n in pieces:
        assert s is None or (pos % 16 == 0 and s % 8 == 0)
        pos += n
    kb = min(PREP_COLS, K)
    return pl.pallas_call(
        functools.partial(_reorder_rows_kernel, pieces=tuple(pieces)),
        out_shape=jax.ShapeDtypeStruct((n_out, K), BF16),
        grid=(K // kb,),
        in_specs=[pl.BlockSpec((n_src, kb), lambda i: (0, i))],
        out_specs=pl.BlockSpec((n_out, kb), lambda i: (0, i)),
        compiler_params=pltpu.CompilerParams(dimension_semantics=("arbitrary",), vmem_limit_bytes=VMEM_LIMIT),
        name="reorder_rows",
    )(wt)


def _inproj_kernel(x_ref, wt_ref, o_ref, xb_ref):
    @pl.when(pl.program_id(1) == 0)
    def _():
        xb_ref[...] = x_ref[...].astype(BF16)

    o_ref[...] = lax.dot_general(xb_ref[...], wt_ref[...], _DIMS["nt"], preferred_element_type=F32)


def _inproj(x2d, wt_bf16, tm, tn):
    T, D = x2d.shape
    N = wt_bf16.shape[0]
    return pl.pallas_call(
        _inproj_kernel,
        out_shape=jax.ShapeDtypeStruct((T, N), F32),
        grid=(T // tm, N // tn),
        in_specs=[
            pl.BlockSpec((tm, D), lambda i, j: (i, 0)),
            pl.BlockSpec((tn, D), lambda i, j: (j, 0)),
        ],
        out_specs=pl.BlockSpec((tm, tn), lambda i, j: (i, j)),
        scratch_shapes=[pltpu.VMEM((tm, D), BF16)],
        compiler_params=pltpu.CompilerParams(
            dimension_semantics=("arbitrary", "arbitrary"), vmem_limit_bytes=VMEM_LIMIT),
        name="inproj",
    )(x2d, wt_bf16)


def _conv_silu(cur_ref, prev_ref, w_ref, b_ref):
    cur = cur_ref[...]
    prev8 = prev_ref[...]
    q = cur.shape[0]
    acc = b_ref[...] + w_ref[3:4, :] * cur
    for j in (1, 2, 3):
        acc = acc + w_ref[3 - j:4 - j, :] * _shift_rows(cur, prev8, j)
    prev_ref[...] = cur[q - 8:, :]
    return acc * jax.nn.sigmoid(acc)


def _ssd_kernel(z_ref, xs_ref, bc_ref, sm_ref, cwx_ref, cbx_ref, cwbc_ref, cbbc_ref,
                dtb_ref, alog_ref, dsk_ref, ng_ref, e_ref, o_ref,
                st_ref, px_ref, pbc_ref, xc_ref, bcc_ref, ex_ref, *, G, H, dt_off, P):
    Q = xs_ref.shape[0]
    N = NSTATE
    GW = 4 * HEAD

    @pl.when(pl.program_id(1) == 0)
    def _():
        st_ref[...] = jnp.zeros_like(st_ref)
        px_ref[...] = jnp.zeros_like(px_ref)
        pbc_ref[...] = jnp.zeros_like(pbc_ref)

    xc_ref[...] = _conv_silu(xs_ref, px_ref, cwx_ref, cbx_ref)
    bcc_ref[...] = _conv_silu(bc_ref, pbc_ref, cwbc_ref, cbbc_ref)

    tile = dt_off // 128
    dt_lane = dt_off % 128
    lane = lax.broadcasted_iota(jnp.int32, (1, 128), 1)
    hmask = (lane >= dt_lane) & (lane < dt_lane + H)
    raw = sm_ref[:, tile * 128:(tile + 1) * 128]
    dt = jnp.where(hmask, _softplus(raw + dtb_ref[...]), 0.0)
    a_neg = jnp.where(hmask, -jnp.exp(alog_ref[...]), 0.0)
    a = dt * a_neg
    row = lax.broadcasted_iota(jnp.int32, (Q, Q), 0)
    col = lax.broadcasted_iota(jnp.int32, (Q, Q), 1)
    causal = col <= row
    tril = jnp.where(causal, 1.0, 0.0).astype(BF16)
    a_cs = _mm(tril, a, "nn", 1, 3)
    a_last = a_cs[Q - 1:Q, :]
    a_cs_t = a_cs.T
    stack = jnp.concatenate(
        [dt, jnp.exp(a_cs), jnp.exp(a_last - a_cs), jnp.broadcast_to(jnp.exp(a_last), (8, 128))], axis=0)
    ex_ref[...] = _mm(stack, e_ref[...], "nn", 3, 1)

    lane_p = lax.broadcasted_iota(jnp.int32, (1, PAIR), 1)
    m0 = lane_p < HEAD
    for g in range(G):
        gs = slice(g * GW, (g + 1) * GW)
        bg = bcc_ref[:, g * N:(g + 1) * N]
        cg = bcc_ref[:, G * N + g * N:G * N + (g + 1) * N]
        scores = _mm(cg, bg, "nt", P, P)
        xsg = xc_ref[:, gs]
        xdt = xsg * ex_ref[0:Q, gs]
        st = st_ref[g]
        y = _mm(cg, st, "nn", P, P) * ex_ref[Q:2 * Q, gs]
        ypairs = []
        for pr in range(2):
            xp = xdt[:, pr * PAIR:(pr + 1) * PAIR]
            acc = None
            for half in range(2):
                hl = dt_lane + g * 4 + pr * 2 + half
                seg = a_cs[:, hl:hl + 1] - a_cs_t[hl:hl + 1, :]
                lm = jnp.exp(jnp.where(causal, seg, -jnp.inf))
                xm = jnp.where(m0, xp, 0.0) if half == 0 else jnp.where(m0, 0.0, xp)
                t = _mm(scores * lm, xm, "nn", P, P)
                acc = t if acc is None else acc + t
            ypairs.append(acc)
        y = y + jnp.concatenate(ypairs, axis=1)
        st_ref[g] = st * ex_ref[3 * Q:3 * Q + 1, gs] + _mm(bg.T, xdt * ex_ref[2 * Q:3 * Q, gs], "nn", P, P)
        y = y + dsk_ref[:, gs] * xsg
        zg = z_ref[:, gs]
        v = y * (zg * jax.nn.sigmoid(zg))
        ms = jnp.mean(v * v, axis=-1, keepdims=True)
        o_ref[:, gs] = (v * lax.rsqrt(ms + SSM_NORM_EPS) * ng_ref[:, gs]).astype(BF16)


def _ssd(u, B, L, D_SSM, G, H, dt_off, small_blk, cwx, cbx, cwbc, cbbc, dtb, alog, dsk, ng, emat, P, riders):
    Q = SSD_CHUNK
    NC = L // Q
    T = B * L
    full = lambda shape: pl.BlockSpec(shape, lambda b, c: (0,) * len(shape))
    rowblk = lambda w, j: pl.BlockSpec((Q, w), lambda b, c: (b * NC + c, j))
    rider_specs, rider_shapes = _rider_specs(riders, B * NC, lambda b, c: (b * NC + c, 0))
    body = functools.partial(_ssd_kernel, G=G, H=H, dt_off=dt_off, P=P)
    res = pl.pallas_call(
        _with_cast_riders(body, 13, 1, len(riders)),
        out_shape=[jax.ShapeDtypeStruct((T, D_SSM), BF16)] + rider_shapes,
        grid=(B, NC),
        in_specs=[
            rowblk(D_SSM, 0), rowblk(D_SSM, 1), rowblk(D_SSM, 2), rowblk(SMALL_W, small_blk),
            full(cwx.shape), full(cbx.shape), full(cwbc.shape), full(cbbc.shape),
            full(dtb.shape), full(alog.shape), full(dsk.shape), full(ng.shape), full(emat.shape),
        ] + rider_specs,
        out_specs=[pl.BlockSpec((Q, D_SSM), lambda b, c: (b * NC + c, 0))] + rider_specs,
        scratch_shapes=[
            pltpu.VMEM((G, NSTATE, 4 * HEAD), F32),
            pltpu.VMEM((8, D_SSM), F32),
            pltpu.VMEM((8, D_SSM), F32),
            pltpu.VMEM((Q, D_SSM), F32),
            pltpu.VMEM((Q, D_SSM), F32),
            pltpu.VMEM((3 * Q + 8, D_SSM), F32),
        ],
        compiler_params=pltpu.CompilerParams(
            dimension_semantics=("arbitrary", "arbitrary"), vmem_limit_bytes=VMEM_LIMIT),
        name="ssd",
    )(u, u, u, u, cwx, cbx, cwbc, cbbc, dtb, alog, dsk, ng, emat, *riders)
    return res[0], res[1:]


def _rwkv_kernel(r_ref, k_ref, v_ref, sm_ref, mur_ref, muk_ref, muv_ref, mus_ref,
                 w0_ref, a0_ref, kk_ref, ka_ref, rk_ref, gng_ref, gnb_ref,
                 wd_ref, wa_ref, wg_ref, o_ref,
                 st_ref, pr_ref, pk_ref, pv_ref, ps_ref, th_ref, xa_ref, sg_ref, *, NP, UNR, P, PL):
    C = r_ref.shape[0]

    @pl.when(pl.program_id(1) == 0)
    def _():
        st_ref[...] = jnp.zeros_like(st_ref)
        pr_ref[...] = jnp.zeros_like(pr_ref)
        pk_ref[...] = jnp.zeros_like(pk_ref)
        pv_ref[...] = jnp.zeros_like(pv_ref)
        ps_ref[...] = jnp.zeros_like(ps_ref)

    def shift_lerp(cur, prev8, mu):
        prev = _shift_rows(cur, prev8, 1)
        return cur + (prev - cur) * mu

    sm = sm_ref[...]
    ss = shift_lerp(sm, ps_ref[...], mus_ref[...])
    ps_ref[...] = sm[C - 8:, :]
    th_ref[...] = jnp.tanh(ss[:, 0:128])
    xa_ref[...] = ss[:, 0:256]
    sg_ref[...] = jax.nn.sigmoid(ss[:, 128:512])

    lane = lax.broadcasted_iota(jnp.int32, (C, PAIR), 1)
    rowi = lax.broadcasted_iota(jnp.int32, (C, PAIR), 0)
    m0 = lane < HEAD
    s_idx = jnp.where(m0, lane, lane - HEAD)
    strict = s_idx < rowi
    incl = s_idx <= rowi
    eye_p = jnp.where(s_idx == rowi, 1.0, 0.0)
    r2 = lax.broadcasted_iota(jnp.int32, (PAIR, PAIR), 0)
    c2 = lax.broadcasted_iota(jnp.int32, (PAIR, PAIR), 1)
    ones_bd = jnp.where((r2 < HEAD) == (c2 < HEAD), 1.0, 0.0).astype(BF16)
    r3 = lax.broadcasted_iota(jnp.int32, (C, C), 0)
    c3 = lax.broadcasted_iota(jnp.int32, (C, C), 1)
    tril = jnp.where(c3 <= r3, 1.0, 0.0).astype(BF16)
    n_dbl = C.bit_length() - 2

    def one_pair(ds):
        r_raw = r_ref[:, ds]
        k_raw = k_ref[:, ds]
        v_raw = v_ref[:, ds]
        s_r = shift_lerp(r_raw, pr_ref[:, ds], mur_ref[:, ds])
        s_k = shift_lerp(k_raw, pk_ref[:, ds], muk_ref[:, ds])
        v = shift_lerp(v_raw, pv_ref[:, ds], muv_ref[:, ds])
        s0 = st_ref[:, ds]
        w0, a0, ka, rk = w0_ref[:, ds], a0_ref[:, ds], ka_ref[:, ds], rk_ref[:, ds]
        gng, gnb = gng_ref[:, ds], gnb_ref[:, ds]
        kkr = s_k * kk_ref[:, ds]
        lw = _mm(th_ref[...], wd_ref[:, ds], "nn", PL, PL)
        la = _mm(xa_ref[...], wa_ref[:, ds], "nn", PL, PL)
        gate = _mm(sg_ref[...], wg_ref[:, ds], "nn", PL, PL)
        n2 = _mm(kkr * kkr, ones_bd, "nn", SUM_PIECES, 1)
        bdv = _block_diag(v, m0)
        yield
        ld = -jnp.exp(-_softplus(-(w0 + lw)) - 0.5)
        a_sig = jax.nn.sigmoid(a0 + la)
        kk = kkr / jnp.maximum(jnp.sqrt(n2), L2_EPS)
        k2 = s_k * (1.0 + (a_sig - 1.0) * ka)
        bvec = kk * a_sig
        cs = _mm(tril, ld, "nn", 1, CUMSUM_PIECES)
        bonus = _mm(s_r * k2 * rk, ones_bd, "nn", SUM_PIECES, 1)
        yield
        cs_last = cs[C - 1:C, :]
        e_in = jnp.exp(-cs)
        e_end = jnp.exp(cs_last - cs)
        at = -kk * jnp.exp(cs - ld)
        rt = s_r * jnp.exp(cs)
        lhs = jnp.concatenate([at, rt], axis=0)
        rhs = jnp.concatenate(
            [_block_diag(k2 * e_in, m0), _block_diag(bvec * e_in, m0), _block_diag(s0, m0)], axis=0)
        akbs = _mm(lhs, rhs, "nt", P, P)
        ak, ab, ss0 = akbs[:, 0:PAIR], akbs[:, PAIR:2 * PAIR], akbs[:, 2 * PAIR:3 * PAIR]
        kb = jnp.concatenate([k2 * e_end, bvec * e_end], axis=0)
        yield
        a_ak = jnp.where(strict, ak[0:C], 0.0)
        a_rk = jnp.where(incl, ak[C:2 * C], 0.0)
        a_rb = jnp.where(incl, ab[C:2 * C], 0.0)
        m = jnp.where(strict, ab[0:C], 0.0)
        tinv = eye_p + m
        m = _mm(m, _block_diag(m, m0), "nn", P, P)
        akv = _mm(jnp.concatenate([a_ak, a_rk], axis=0), bdv, "nn", P, P)
        ru = ss0[0:C] + akv[0:C]
        y0 = ss0[C:2 * C] + akv[C:2 * C]
        yield
        for _ in range(n_dbl - 1):
            tm_ = _mm(jnp.concatenate([tinv, m], axis=0), _block_diag(m, m0), "nn", P, P)
            tinv = tinv + tm_[0:C]
            m = tm_[C:2 * C]
            yield
        tinv = tinv + _mm(tinv, _block_diag(m, m0), "nn", P, P)
        yield
        u = _mm(tinv, _block_diag(ru, m0), "nn", P, P)
        yield
        y = y0 + _mm(a_rb, _block_diag(u, m0), "nn", P, P)
        vu = jnp.concatenate([v, u], axis=0)
        dl = _mm(vu.T, kb, "nn", P, P)
        yield
        inv_n = 1.0 / HEAD
        mean = _mm(y, ones_bd, "nn", SUM_PIECES, 1) * inv_n
        s_new = s0 * jnp.exp(cs_last) + jnp.where(m0, dl[0:HEAD], dl[HEAD:PAIR])
        yield
        d = y - mean
        var = _mm(d * d, ones_bd, "nn", SUM_PIECES, 1) * inv_n
        yield
        yn = d * lax.rsqrt(var + RWKV_GN_EPS) * gng + gnb
        out = ((yn + bonus * v) * gate).astype(BF16)
        yield
        pr_ref[:, ds] = r_raw[C - 8:, :]
        pk_ref[:, ds] = k_raw[C - 8:, :]
        pv_ref[:, ds] = v_raw[C - 8:, :]
        st_ref[:, ds] = s_new
        o_ref[:, ds] = out

    def body(i, carry):
        gens = [one_pair(pl.ds(pl.multiple_of((i * UNR + k) * PAIR, PAIR), PAIR)) for k in range(UNR)]
        while gens:
            gens = [g for g in gens if next(g, "done") != "done"]
        return carry

    lax.fori_loop(0, NP // UNR, body, 0)


def _rwkv(u, B, L, D_R, rkv_blk, small_blk, params, P, PL, UNR, riders):
    C = RWKV_CHUNK
    NC = L // C
    T = B * L
    NP = D_R // PAIR
    assert NP % UNR == 0
    full = lambda a: pl.BlockSpec(a.shape, lambda b, c: (0,) * a.ndim)
    rowblk = lambda w, j: pl.BlockSpec((C, w), lambda b, c: (b * NC + c, j))
    rider_specs, rider_shapes = _rider_specs(riders, B * NC, lambda b, c: (b * NC + c, 0))
    body = functools.partial(_rwkv_kernel, NP=NP, UNR=UNR, P=P, PL=PL)
    res = pl.pallas_call(
        _with_cast_riders(body, 4 + len(params), 1, len(riders)),
        out_shape=[jax.ShapeDtypeStruct((T, D_R), BF16)] + rider_shapes,
        grid=(B, NC),
        in_specs=[rowblk(D_R, rkv_blk), rowblk(D_R, rkv_blk + 1), rowblk(D_R, rkv_blk + 2),
                  rowblk(SMALL_W, small_blk)] + [full(a) for a in params] + rider_specs,
        out_specs=[pl.BlockSpec((C, D_R), lambda b, c: (b * NC + c, 0))] + rider_specs,
        scratch_shapes=[
            pltpu.VMEM((HEAD, D_R), F32),
            pltpu.VMEM((8, D_R), F32), pltpu.VMEM((8, D_R), F32), pltpu.VMEM((8, D_R), F32),
            pltpu.VMEM((8, SMALL_W), F32),
            pltpu.VMEM((C, 128), F32), pltpu.VMEM((C, 256), F32), pltpu.VMEM((C, 384), F32),
        ],
        compiler_params=pltpu.CompilerParams(
            dimension_semantics=("arbitrary", "arbitrary"), vmem_limit_bytes=VMEM_LIMIT),
        name="rwkv",
    )(u, u, u, u, *params, *riders)
    return res[0], res[1:]


LN_ROWS = 128


def _layer_norm_rows(o_ref, g_ref, b_ref, ob_ref=None):
    tm = o_ref.shape[0]

    def body(i, carry):
        rs = pl.ds(pl.multiple_of(i * LN_ROWS, LN_ROWS), LN_ROWS)
        h = o_ref[rs, :]
        mu = jnp.mean(h, axis=-1, keepdims=True)
        d = h - mu
        var = jnp.mean(d * d, axis=-1, keepdims=True)
        y = d * lax.rsqrt(var + LN_EPS) * g_ref[...] + b_ref[...]
        o_ref[rs, :] = y
        if ob_ref is not None:
            ob_ref[rs, :] = y.astype(BF16)
        return carry

    lax.fori_loop(0, tm // LN_ROWS, body, 0)


_LN_PARAMS = pltpu.CompilerParams(dimension_semantics=("arbitrary", "arbitrary"), vmem_limit_bytes=VMEM_LIMIT)


def _outproj_ln_kernel(y1_ref, y2_ref, w_ref, res_ref, g_ref, b_ref, o_ref, ob_ref, *, alpha, tn, nj):
    j = pl.program_id(1)
    k1 = y1_ref.shape[1]
    acc = jnp.dot(y1_ref[...], w_ref[0:k1, :], preferred_element_type=F32)
    acc = acc + jnp.dot(y2_ref[...], w_ref[k1:, :], preferred_element_type=F32)
    cs = pl.ds(pl.multiple_of(j * tn, tn), tn)
    o_ref[:, cs] = alpha * res_ref[...] + acc

    @pl.when(j == nj - 1)
    def _():
        _layer_norm_rows(o_ref, g_ref, b_ref, ob_ref)


def _outproj_ln(y1, y2, w, res, g, b, alpha, tm, tn):
    T, K1 = y1.shape
    K2 = y2.shape[1]
    D = w.shape[1]
    assert w.shape[0] == K1 + K2
    nj = D // tn
    return pl.pallas_call(
        functools.partial(_outproj_ln_kernel, alpha=alpha, tn=tn, nj=nj),
        out_shape=(jax.ShapeDtypeStruct((T, D), F32), jax.ShapeDtypeStruct((T, D), BF16)),
        grid=(T // tm, nj),
        in_specs=[
            pl.BlockSpec((tm, K1), lambda i, j: (i, 0)),
            pl.BlockSpec((tm, K2), lambda i, j: (i, 0)),
            pl.BlockSpec((K1 + K2, tn), lambda i, j: (0, j)),
            pl.BlockSpec((tm, tn), lambda i, j: (i, j)),
            pl.BlockSpec((1, D), lambda i, j: (0, 0)),
            pl.BlockSpec((1, D), lambda i, j: (0, 0)),
        ],
        out_specs=(pl.BlockSpec((tm, D), lambda i, j: (i, 0)), pl.BlockSpec((tm, D), lambda i, j: (i, 0))),
        compiler_params=_LN_PARAMS,
        name="outproj_ln",
    )(y1, y2, w, res, g, b)


def _ffn_ln_kernel(hb_ref, res_ref, wu_ref, wd_ref, g_ref, b_ref, o_ref, ob_ref, *, alpha, nf, tr):
    f = pl.program_id(1)

    @pl.when(f == 0)
    def _():
        o_ref[...] = jnp.zeros_like(o_ref)

    a = jnp.dot(hb_ref[...], wu_ref[...], preferred_element_type=F32)
    a = jnp.maximum(a, 0.0)
    a = (a * a).astype(BF16)
    o_ref[...] += jnp.dot(a, wd_ref[...], preferred_element_type=F32)

    cs = pl.ds(pl.multiple_of(f * tr, tr), tr)
    o_ref[:, cs] += alpha * res_ref[...]

    @pl.when(f == nf - 1)
    def _():
        _layer_norm_rows(o_ref, g_ref, b_ref, ob_ref)


def _ffn_ln(hb, h, wu, wd, g, b, alpha, tm, tf):
    T, D = h.shape
    F = wu.shape[1]
    nf = F // tf
    tr = D // nf
    assert tr % 128 == 0
    return pl.pallas_call(
        functools.partial(_ffn_ln_kernel, alpha=alpha, nf=nf, tr=tr),
        out_shape=(jax.ShapeDtypeStruct((T, D), F32), jax.ShapeDtypeStruct((T, D), BF16)),
        grid=(T // tm, nf),
        in_specs=[
            pl.BlockSpec((tm, D), lambda i, f: (i, 0)),
            pl.BlockSpec((tm, tr), lambda i, f: (i, f)),
            pl.BlockSpec((D, tf), lambda i, f: (0, f)),
            pl.BlockSpec((tf, D), lambda i, f: (f, 0)),
            pl.BlockSpec((1, D), lambda i, f: (0, 0)),
            pl.BlockSpec((1, D), lambda i, f: (0, 0)),
        ],
        out_specs=(pl.BlockSpec((tm, D), lambda i, f: (i, 0)), pl.BlockSpec((tm, D), lambda i, f: (i, 0))),
        compiler_params=_LN_PARAMS,
        name="ffn_ln",
    )(hb, h, wu, wd, g, b)


def _ple_ln_kernel(hb_ref, res_ref, p_ref, wg_ref, wp_ref, g_ref, b_ref, o_ref, *, alpha, tn, nj):
    j = pl.program_id(1)
    gate = jax.nn.sigmoid(jnp.dot(hb_ref[...], wg_ref[...], preferred_element_type=F32))
    emb = jnp.dot(p_ref[...].astype(BF16), wp_ref[...], preferred_element_type=F32)
    cs = pl.ds(pl.multiple_of(j * tn, tn), tn)
    o_ref[:, cs] = alpha * res_ref[...] + emb * gate

    @pl.when(j == nj - 1)
    def _():
        _layer_norm_rows(o_ref, g_ref, b_ref)


def _ple_ln(hb, h, p2d, wg, wp, g, b, alpha, tm, tn):
    T, D = h.shape
    DP = p2d.shape[1]
    nj = D // tn
    return pl.pallas_call(
        functools.partial(_ple_ln_kernel, alpha=alpha, tn=tn, nj=nj),
        out_shape=jax.ShapeDtypeStruct((T, D), F32),
        grid=(T // tm, nj),
        in_specs=[
            pl.BlockSpec((tm, D), lambda i, j: (i, 0)),
            pl.BlockSpec((tm, tn), lambda i, j: (i, j)),
            pl.BlockSpec((tm, DP), lambda i, j: (i, 0)),
            pl.BlockSpec((D, tn), lambda i, j: (0, j)),
            pl.BlockSpec((DP, tn), lambda i, j: (0, j)),
            pl.BlockSpec((1, D), lambda i, j: (0, 0)),
            pl.BlockSpec((1, D), lambda i, j: (0, 0)),
        ],
        out_specs=pl.BlockSpec((tm, D), lambda i, j: (i, 0)),
        compiler_params=_LN_PARAMS,
        name="ple_ln",
    )(hb, h, p2d, wg, wp, g, b)


def _pad_rows(w, rows, offset=0):
    out = jnp.zeros((rows, w.shape[1]), w.dtype)
    return out.at[offset:offset + w.shape[0]].set(w)


def _layer(h, p_i, w_in, conv_w, conv_b, dt_bias, A_log, D_skip, ssm_norm_g,
           rwkv_mu, w0, w_decay_b, a0, w_aaa_b, w_gate_b, k_k, k_a, r_k, gn_g, gn_b,
           w_out, ln1_g, ln1_b, w_up, w_down, ln2_g, ln2_b, w_ple, w_ple_gate, ln3_g, ln3_b, alpha):
    B, L, D = h.shape
    T = B * L
    D_SSM = ssm_norm_g.shape[0]
    H = dt_bias.shape[0]
    D_XBC = conv_w.shape[1]
    G = (D_XBC - D_SSM) // (2 * NSTATE)
    D_R = w0.shape[0]
    dl, al, gl = w_decay_b.shape[0], w_aaa_b.shape[0], w_gate_b.shape[0]
    dt_off = dl + al + gl
    assert D_XBC == 2 * D_SSM and D_R == D_SSM and D_SSM == H * HEAD and H == 4 * G
    assert dl <= 128 and dl + al <= 256 and 128 <= dl + al and dt_off + H <= SMALL_W
    assert dt_off // 128 == (dt_off + H - 1) // 128
    assert L % SSD_CHUNK == 0 and L % RWKV_CHUNK == 0 and D_R % PAIR == 0

    o_dt = D_SSM + D_XBC
    o_r = o_dt + H
    pad = SMALL_W - (dt_off + H)
    wt_cat = _reorder_rows(w_in.T, [(0, o_dt), (o_r, 3 * D_R + dt_off), (o_dt, H), (None, pad)])
    small_blk = (D_SSM + D_XBC + 3 * D_R) // SMALL_W
    rkv_blk = (D_SSM + D_XBC) // D_R

    tm = min(512, T)
    n_cat = wt_cat.shape[0]
    tn_in = next(t for t in (1280, 1024, 512) if n_cat % t == 0)
    tn = min(1024, D)
    u = _inproj(h.reshape(T, D), wt_cat, tm, tn_in)

    row = lambda a: a.reshape(1, -1).astype(F32)
    lane_tile = lambda a: jnp.zeros((1, 128), F32).at[0, dt_off % 128:dt_off % 128 + H].set(a)
    hl = jnp.arange(128)[:, None] - dt_off % 128
    emat = (hl == (jnp.arange(D_SSM)[None, :] // HEAD)).astype(BF16)
    y_ssm, (wo, wpg) = _ssd(
        u, B, L, D_SSM, G, H, dt_off, small_blk,
        conv_w[:, :D_SSM], row(conv_b[:D_SSM]), conv_w[:, D_SSM:], row(conv_b[D_SSM:]),
        lane_tile(dt_bias), lane_tile(A_log), row(jnp.repeat(D_skip, HEAD)), row(ssm_norm_g), emat,
        P=SSD_PIECES, riders=[w_out, w_ple_gate])

    mu = rwkv_mu
    mus = jnp.concatenate([mu[3 * D_R:], jnp.zeros((SMALL_W - dt_off,), F32)])
    params = [row(mu[:D_R]), row(mu[D_R:2 * D_R]), row(mu[2 * D_R:3 * D_R]), row(mus),
              row(w0), row(a0), row(k_k), row(k_a), row(r_k), row(gn_g), row(gn_b),
              _pad_rows(w_decay_b, 128), _pad_rows(w_aaa_b, 256, dl), _pad_rows(w_gate_b, 384, dl + al - 128)]
    y_rwkv, (wu, wdn) = _rwkv(u, B, L, D_R, rkv_blk, small_blk, params, P=RWKV_PIECES, PL=LORA_PIECES,
                              UNR=min(RWKV_UNROLL, D_R // PAIR), riders=[w_up, w_down])

    h1, h1b = _outproj_ln(y_ssm, y_rwkv, wo, h.reshape(T, D), row(ln1_g), row(ln1_b), alpha, tm, tn)
    h2, h2b = _ffn_ln(h1b, h1, wu, wdn, row(ln2_g), row(ln2_b), alpha, tm, 512)
    h3 = _ple_ln(h2b, h2, p_i.reshape(T, -1), wpg, w_ple.astype(BF16), row(ln3_g), row(ln3_b), alpha, tm, tn)
    return h3.reshape(B, L, D)


def kernel(x, p, w_in, conv_w, conv_b, dt_bias, A_log, D_skip, ssm_norm_g, rwkv_mu, w0, w_decay_b, a0, w_aaa_b, w_gate_b, k_k, k_a, r_k, gn_g, gn_b, w_out, ln1_g, ln1_b, w_up, w_down, ln2_g, ln2_b, w_ple, w_ple_gate, ln3_g, ln3_b):
    depth = w_in.shape[0]
    alpha = float((2 * depth) ** 0.25)
    h = x
    for i in range(depth):
        h = _layer(h, p[i], w_in[i], conv_w[i], conv_b[i], dt_bias[i], A_log[i], D_skip[i], ssm_norm_g[i],
                   rwkv_mu[i], w0[i], w_decay_b[i], a0[i], w_aaa_b[i], w_gate_b[i], k_k[i], k_a[i],
                   r_k[i].reshape(-1), gn_g[i], gn_b[i], w_out[i], ln1_g[i], ln1_b[i], w_up[i], w_down[i],
                   ln2_g[i], ln2_b[i], w_ple[i], w_ple_gate[i], ln3_g[i], ln3_b[i], alpha)
    return h
```

```python
import functools
import math

import jax
import jax.numpy as jnp
from jax import lax
from jax.experimental import pallas as pl
from jax.experimental.pallas import tpu as pltpu

F32 = jnp.float32
BF16 = jnp.bfloat16

LANES = 128
SUBLANES = 8
BF16_ROWS = 16
HEAD = 64
PAIR = 2 * HEAD
NSTATE = 128
SSD_CHUNK = 128
RWKV_CHUNK = 64
RWKV_BLOCK_ROWS = 128
SMALL_W = 512
SSM_NORM_EPS = 1e-5
RWKV_GN_EPS = 64e-5
L2_EPS = 1e-12
DECAY_SCALE = math.exp(-0.5)
LN_EPS = 1e-5
VMEM_LIMIT = 60 * 1024 * 1024
ROW_TILE = 512
INPROJ_COL_TILES = (1280, 1024, 512)
LN_COL_TILE = 1024
FFN_TILE = 512
SSD_PIECES = 1
RWKV_PIECES = 1
LORA_PIECES = 1
SUM_PIECES = 1
CUMSUM_PIECES = 2
RWKV_UNROLL = 16

_DIMS = {
    "nn": (((1,), (0,)), ((), ())),
    "nt": (((1,), (1,)), ((), ())),
    "tn": (((0,), (0,)), ((), ())),
}


def _split(x, n):
    if x.dtype == BF16:
        return [x]
    parts = []
    rem = x
    for i in range(n):
        p = rem.astype(BF16)
        parts.append(p)
        if i + 1 < n:
            rem = rem - p.astype(F32)
    return parts


def _mm(a, b, dims="nn", pa=1, pb=1):
    aps = _split(a, pa)
    bps = _split(b, pb)
    lim = max(len(aps), len(bps))
    out = None
    for i in reversed(range(len(aps))):
        for j in reversed(range(len(bps))):
            if i + j < lim:
                t = lax.dot_general(aps[i], bps[j], _DIMS[dims], preferred_element_type=F32)
                out = t if out is None else out + t
    return out


def _softplus(x):
    return jnp.maximum(x, 0.0) + jnp.log(1.0 + jnp.exp(-jnp.abs(x)))


def _shift_rows(cur, prev8, j):
    rc = pltpu.roll(cur, j, 0)
    rp = pltpu.roll(prev8, j, 0)
    row = lax.broadcasted_iota(jnp.int32, rp.shape, 0)
    first = jnp.where(row < j, rp, rc[0:SUBLANES, :])
    return jnp.concatenate([first, rc[SUBLANES:, :]], axis=0)


def _block_diag(x, m0):
    return jnp.concatenate([jnp.where(m0, x, 0.0), jnp.where(m0, 0.0, x)], axis=0)


def _with_cast_riders(body, n_in, n_out, n_riders):
    def wrapped(*refs):
        ins, rest = refs[:n_in], refs[n_in:]
        rin, rest = rest[:n_riders], rest[n_riders:]
        outs, rest = rest[:n_out], rest[n_out:]
        rout, scratch = rest[:n_riders], rest[n_riders:]
        for src, dst in zip(rin, rout):
            dst[...] = src[...].astype(BF16)
        body(*ins, *outs, *scratch)
    return wrapped


def _rider_specs(riders, n_steps, index_map):
    specs, shapes = [], []
    for w in riders:
        rows = w.shape[0] // n_steps
        assert rows * n_steps == w.shape[0] and rows % BF16_ROWS == 0
        specs.append(pl.BlockSpec((rows, w.shape[1]), index_map))
        shapes.append(jax.ShapeDtypeStruct(w.shape, BF16))
    return specs, shapes


PREP_COLS = 256


def _reorder_rows_kernel(w_ref, o_ref, *, pieces):
    pos = 0
    for s, n in pieces:
        if s is None:
            o_ref[pos:pos + n, :] = jnp.zeros((n, o_ref.shape[1]), BF16)
        else:
            o_ref[pos:pos + n, :] = w_ref[s:s + n, :].astype(BF16)
        pos += n


def _reorder_rows(wt, pieces):
    n_src, K = wt.shape
    n_out = sum(n for _, n in pieces)
    pos = 0
    for s, n in pieces:
        assert s is None or (pos % BF16_ROWS == 0 and s % SUBLANES == 0)
        pos += n
    kb = min(PREP_COLS, K)
    return pl.pallas_call(
        functools.partial(_reorder_rows_kernel, pieces=tuple(pieces)),
        out_shape=jax.ShapeDtypeStruct((n_out, K), BF16),
        grid=(K // kb,),
        in_specs=[pl.BlockSpec((n_src, kb), lambda i: (0, i))],
        out_specs=pl.BlockSpec((n_out, kb), lambda i: (0, i)),
        compiler_params=pltpu.CompilerParams(dimension_semantics=("arbitrary",), vmem_limit_bytes=VMEM_LIMIT),
        name="reorder_rows",
    )(wt)


def _inproj_kernel(x_ref, wt_ref, o_ref, xb_ref):
    @pl.when(pl.program_id(1) == 0)
    def _():
        xb_ref[...] = x_ref[...].astype(BF16)

    o_ref[...] = lax.dot_general(xb_ref[...], wt_ref[...], _DIMS["nt"], preferred_element_type=F32)


def _inproj(x2d, wt_bf16, tm, tn):
    T, D = x2d.shape
    N = wt_bf16.shape[0]
    return pl.pallas_call(
        _inproj_kernel,
        out_shape=jax.ShapeDtypeStruct((T, N), F32),
        grid=(T // tm, N // tn),
        in_specs=[
            pl.BlockSpec((tm, D), lambda i, j: (i, 0)),
            pl.BlockSpec((tn, D), lambda i, j: (j, 0)),
        ],
        out_specs=pl.BlockSpec((tm, tn), lambda i, j: (i, j)),
        scratch_shapes=[pltpu.VMEM((tm, D), BF16)],
        compiler_params=pltpu.CompilerParams(
            dimension_semantics=("arbitrary", "arbitrary"), vmem_limit_bytes=VMEM_LIMIT),
        name="inproj",
    )(x2d, wt_bf16)


def _conv_silu(cur_ref, prev_ref, w_ref, b_ref, out_ref):
    q = cur_ref.shape[0]
    for c in range(0, cur_ref.shape[1], LANES):
        cs = slice(c, c + LANES)
        cur = cur_ref[:, cs]
        prev8 = prev_ref[:, cs]
        acc = b_ref[:, cs] + w_ref[3:4, cs] * cur
        for j in (1, 2, 3):
            acc = acc + w_ref[3 - j:4 - j, cs] * _shift_rows(cur, prev8, j)
        prev_ref[:, cs] = cur[q - SUBLANES:, :]
        out_ref[:, cs] = acc * jax.nn.sigmoid(acc)


def _ssd_kernel(z_ref, xs_ref, bc_ref, sm_ref, cwx_ref, cbx_ref, cwbc_ref, cbbc_ref,
                dtb_ref, alog_ref, dsk_ref, ng_ref, e_ref, o_ref,
                st_ref, px_ref, pbc_ref, xc_ref, bcc_ref, ex_ref, *, G, H, dt_off, P):
    Q = xs_ref.shape[0]
    N = NSTATE
    GW = 4 * HEAD

    @pl.when(pl.program_id(1) == 0)
    def _():
        st_ref[...] = jnp.zeros_like(st_ref)
        px_ref[...] = jnp.zeros_like(px_ref)
        pbc_ref[...] = jnp.zeros_like(pbc_ref)

    _conv_silu(xs_ref, px_ref, cwx_ref, cbx_ref, xc_ref)
    _conv_silu(bc_ref, pbc_ref, cwbc_ref, cbbc_ref, bcc_ref)

    tile = dt_off // LANES
    dt_lane = dt_off % LANES
    lane = lax.broadcasted_iota(jnp.int32, (1, LANES), 1)
    hmask = (lane >= dt_lane) & (lane < dt_lane + H)
    raw = sm_ref[:, tile * LANES:(tile + 1) * LANES]
    dt = jnp.where(hmask, _softplus(raw + dtb_ref[...]), 0.0)
    a_neg = jnp.where(hmask, -jnp.exp(alog_ref[...]), 0.0)
    a = dt * a_neg
    row = lax.broadcasted_iota(jnp.int32, (Q, Q), 0)
    col = lax.broadcasted_iota(jnp.int32, (Q, Q), 1)
    causal = col <= row
    tril = jnp.where(causal, 1.0, 0.0).astype(BF16)
    a_cs = _mm(tril, a, "nn", 1, 3)
    a_last = a_cs[Q - 1:Q, :]
    a_cs_t = a_cs.T
    stack = jnp.concatenate(
        [dt, jnp.exp(a_cs), jnp.exp(a_last - a_cs), jnp.broadcast_to(jnp.exp(a_last), (SUBLANES, LANES))], axis=0)
    ex_ref[...] = _mm(stack, e_ref[...], "nn", 3, 1)

    lane_p = lax.broadcasted_iota(jnp.int32, (1, PAIR), 1)
    m0 = lane_p < HEAD
    for g in range(G):
        gs = slice(g * GW, (g + 1) * GW)
        bg = bcc_ref[:, g * N:(g + 1) * N]
        cg = bcc_ref[:, G * N + g * N:G * N + (g + 1) * N]
        scores = _mm(cg, bg, "nt", P, P)
        xsg = xc_ref[:, gs]
        xdt = xsg * ex_ref[0:Q, gs]
        st = st_ref[g]
        y = _mm(cg, st, "nn", P, P) * ex_ref[Q:2 * Q, gs]
        ypairs = []
        for pr in range(2):
            xp = xdt[:, pr * PAIR:(pr + 1) * PAIR]
            acc = None
            for half in range(2):
                hl = dt_lane + g * 4 + pr * 2 + half
                seg = a_cs[:, hl:hl + 1] - a_cs_t[hl:hl + 1, :]
                lm = jnp.exp(jnp.where(causal, seg, -jnp.inf))
                xm = jnp.where(m0, xp, 0.0) if half == 0 else jnp.where(m0, 0.0, xp)
                t = _mm(scores * lm, xm, "nn", P, P)
                acc = t if acc is None else acc + t
            ypairs.append(acc)
        y = y + jnp.concatenate(ypairs, axis=1)
        st_ref[g] = st * ex_ref[3 * Q:3 * Q + 1, gs] + _mm(bg.T, xdt * ex_ref[2 * Q:3 * Q, gs], "nn", P, P)
        y = y + dsk_ref[:, gs] * xsg
        zg = z_ref[:, gs]
        v = y * (zg * jax.nn.sigmoid(zg))
        ms = jnp.mean(v * v, axis=-1, keepdims=True)
        o_ref[:, gs] = (v * lax.rsqrt(ms + SSM_NORM_EPS) * ng_ref[:, gs]).astype(BF16)


def _ssd(u, B, L, D_SSM, G, H, dt_off, small_blk, cwx, cbx, cwbc, cbbc, dtb, alog, dsk, ng, emat, P, riders):
    Q = SSD_CHUNK
    NC = L // Q
    T = B * L
    full = lambda shape: pl.BlockSpec(shape, lambda b, c: (0,) * len(shape))
    rowblk = lambda w, j: pl.BlockSpec((Q, w), lambda b, c: (b * NC + c, j))
    rider_specs, rider_shapes = _rider_specs(riders, B * NC, lambda b, c: (b * NC + c, 0))
    body = functools.partial(_ssd_kernel, G=G, H=H, dt_off=dt_off, P=P)
    res = pl.pallas_call(
        _with_cast_riders(body, 13, 1, len(riders)),
        out_shape=[jax.ShapeDtypeStruct((T, D_SSM), BF16)] + rider_shapes,
        grid=(B, NC),
        in_specs=[
            rowblk(D_SSM, 0), rowblk(D_SSM, 1), rowblk(D_SSM, 2), rowblk(SMALL_W, small_blk),
            full(cwx.shape), full(cbx.shape), full(cwbc.shape), full(cbbc.shape),
            full(dtb.shape), full(alog.shape), full(dsk.shape), full(ng.shape), full(emat.shape),
        ] + rider_specs,
        out_specs=[pl.BlockSpec((Q, D_SSM), lambda b, c: (b * NC + c, 0))] + rider_specs,
        scratch_shapes=[
            pltpu.VMEM((G, NSTATE, 4 * HEAD), F32),
            pltpu.VMEM((SUBLANES, D_SSM), F32),
            pltpu.VMEM((SUBLANES, D_SSM), F32),
            pltpu.VMEM((Q, D_SSM), F32),
            pltpu.VMEM((Q, D_SSM), F32),
            pltpu.VMEM((3 * Q + SUBLANES, D_SSM), F32),
        ],
        compiler_params=pltpu.CompilerParams(
            dimension_semantics=("arbitrary", "arbitrary"), vmem_limit_bytes=VMEM_LIMIT),
        name="ssd",
    )(u, u, u, u, cwx, cbx, cwbc, cbbc, dtb, alog, dsk, ng, emat, *riders)
    return res[0], res[1:]


def _rwkv_kernel(r_ref, k_ref, v_ref, sm_ref, mur_ref, muk_ref, muv_ref, mus_ref,
                 w0_ref, a0_ref, kk_ref, ka_ref, rk_ref, gng_ref, gnb_ref,
                 wd_ref, wa_ref, wg_ref, o_ref,
                 st_ref, pr_ref, pk_ref, pv_ref, ps_ref, th_ref, xa_ref, sg_ref, *, NP, UNR, P, PL, spans):
    R = r_ref.shape[0]
    C = RWKV_CHUNK
    NSUB = R // C

    @pl.when(pl.program_id(1) == 0)
    def _():
        st_ref[...] = jnp.zeros_like(st_ref)
        pr_ref[...] = jnp.zeros_like(pr_ref)
        pk_ref[...] = jnp.zeros_like(pk_ref)
        pv_ref[...] = jnp.zeros_like(pv_ref)
        ps_ref[...] = jnp.zeros_like(ps_ref)

    def shift_lerp(cur, prev8, mu):
        prev = _shift_rows(cur, prev8, 1)
        return cur + (prev - cur) * mu

    sm = sm_ref[...]
    ss = shift_lerp(sm, ps_ref[...], mus_ref[...])
    ps_ref[...] = sm[R - SUBLANES:, :]
    (dlo, dhi), (alo, ahi), (glo, ghi) = spans
    th_ref[...] = jnp.tanh(ss[:, dlo:dhi])
    xa_ref[...] = ss[:, alo:ahi]
    sg_ref[...] = jax.nn.sigmoid(ss[:, glo:ghi])

    lane = lax.broadcasted_iota(jnp.int32, (C, PAIR), 1)
    rowi = lax.broadcasted_iota(jnp.int32, (C, PAIR), 0)
    m0 = lane < HEAD
    s_idx = jnp.where(m0, lane, lane - HEAD)
    strict = s_idx < rowi
    incl = s_idx <= rowi
    eye_p = jnp.where(s_idx == rowi, 1.0, 0.0)
    r2 = lax.broadcasted_iota(jnp.int32, (PAIR, PAIR), 0)
    c2 = lax.broadcasted_iota(jnp.int32, (PAIR, PAIR), 1)
    ones_bd = jnp.where((r2 < HEAD) == (c2 < HEAD), 1.0, 0.0).astype(BF16)
    r3 = lax.broadcasted_iota(jnp.int32, (R, R), 0)
    c3 = lax.broadcasted_iota(jnp.int32, (R, R), 1)
    same_chunk = (r3 < C) == (c3 < C) if NSUB == 2 else (r3 // C) == (c3 // C)
    tril = jnp.where((c3 <= r3) & same_chunk, 1.0, 0.0).astype(BF16)
    n_dbl = C.bit_length() - 2
    inv_n = 1.0 / HEAD

    def one_pair(ds):
        r_raw = r_ref[:, ds]
        k_raw = k_ref[:, ds]
        v_raw = v_ref[:, ds]
        s_r = shift_lerp(r_raw, pr_ref[:, ds], mur_ref[:, ds])
        s_k = shift_lerp(k_raw, pk_ref[:, ds], muk_ref[:, ds])
        v = shift_lerp(v_raw, pv_ref[:, ds], muv_ref[:, ds])
        state = st_ref[:, ds]
        w0, a0, ka, rk = w0_ref[:, ds], a0_ref[:, ds], ka_ref[:, ds], rk_ref[:, ds]
        gng, gnb = gng_ref[:, ds], gnb_ref[:, ds]
        kkr = s_k * kk_ref[:, ds]
        lw = _mm(th_ref[...], wd_ref[:, ds], "nn", PL, PL)
        la = _mm(xa_ref[...], wa_ref[:, ds], "nn", PL, PL)
        gate = _mm(sg_ref[...], wg_ref[:, ds], "nn", PL, PL)
        n2 = _mm(kkr * kkr, ones_bd, "nn", SUM_PIECES, 1)
        yield
        ld = -DECAY_SCALE * jax.nn.sigmoid(w0 + lw)
        a_sig = jax.nn.sigmoid(a0 + la)
        kk = kkr * lax.rsqrt(jnp.maximum(n2, L2_EPS * L2_EPS))
        k2 = s_k * (1.0 + (a_sig - 1.0) * ka)
        bvec = kk * a_sig
        cs_all = _mm(tril, ld, "nn", 1, CUMSUM_PIECES)
        bonus = _mm(s_r * k2 * rk, ones_bd, "nn", SUM_PIECES, 1)
        yield
        w_all = jnp.exp(cs_all)
        w_inv = 1.0 / w_all
        at_all = -kk * (w_all * jnp.exp(-ld))
        rt_all = s_r * w_all
        kt_all = k2 * w_inv
        bt_all = bvec * w_inv
        subs = []
        for c in range(NSUB):
            rows = slice(c * C, (c + 1) * C)
            w_end = w_all[(c + 1) * C - 1:(c + 1) * C, :]
            lhs = jnp.concatenate([at_all[rows], rt_all[rows]], axis=0)
            blocks = [_block_diag(kt_all[rows], m0), _block_diag(bt_all[rows], m0)]
            if c == 0:
                blocks.append(_block_diag(state, m0))
            akb = _mm(lhs, jnp.concatenate(blocks, axis=0), "nt", P, P)
            subs.append(dict(
                lhs=lhs, akb=akb, v=v[rows], w_end=w_end,
                ss0=akb[:, 2 * PAIR:3 * PAIR] if c == 0 else None,
                kb=jnp.concatenate([kt_all[rows], bt_all[rows]], axis=0) * w_end))
        yield
        for sb in subs:
            ak, ab = sb["akb"][:, 0:PAIR], sb["akb"][:, PAIR:2 * PAIR]
            a_ak = jnp.where(strict, ak[0:C], 0.0)
            a_rk = jnp.where(incl, ak[C:2 * C], 0.0)
            sb["a_rb"] = jnp.where(incl, ab[C:2 * C], 0.0)
            m = jnp.where(strict, ab[0:C], 0.0)
            sb["tinv"] = eye_p + m
            sb["m"] = _mm(m, _block_diag(m, m0), "nn", P, P)
            sb["akv"] = _mm(jnp.concatenate([a_ak, a_rk], axis=0), _block_diag(sb["v"], m0), "nn", P, P)
        yield
        for _ in range(n_dbl - 1):
            for sb in subs:
                tm_ = _mm(jnp.concatenate([sb["tinv"], sb["m"]], axis=0), _block_diag(sb["m"], m0), "nn", P, P)
                sb["tinv"] = sb["tinv"] + tm_[0:C]
                sb["m"] = tm_[C:2 * C]
            yield
        for sb in subs:
            sb["tinv"] = sb["tinv"] + _mm(sb["tinv"], _block_diag(sb["m"], m0), "nn", P, P)
        yield
        ys = []
        for sb in subs:
            ss0 = sb["ss0"]
            if ss0 is None:
                ss0 = _mm(sb["lhs"], _block_diag(state, m0), "nt", P, P)
                yield
            u = _mm(sb["tinv"], _block_diag(ss0[0:C] + sb["akv"][0:C], m0), "nn", P, P)
            yield
            ys.append(ss0[C:2 * C] + sb["akv"][C:2 * C] + _mm(sb["a_rb"], _block_diag(u, m0), "nn", P, P))
            dl = _mm(jnp.concatenate([sb["v"], u], axis=0).T, sb["kb"], "nn", P, P)
            yield
            state = state * sb["w_end"] + jnp.where(m0, dl[0:HEAD], dl[HEAD:PAIR])
        y = ys[0] if NSUB == 1 else jnp.concatenate(ys, axis=0)
        mean = _mm(y, ones_bd, "nn", SUM_PIECES, 1) * inv_n
        yield
        d = y - mean
        var = _mm(d * d, ones_bd, "nn", SUM_PIECES, 1) * inv_n
        yield
        yn = d * lax.rsqrt(var + RWKV_GN_EPS) * gng + gnb
        out = ((yn + bonus * v) * gate).astype(BF16)
        yield
        pr_ref[:, ds] = r_raw[R - SUBLANES:, :]
        pk_ref[:, ds] = k_raw[R - SUBLANES:, :]
        pv_ref[:, ds] = v_raw[R - SUBLANES:, :]
        st_ref[:, ds] = state
        o_ref[:, ds] = out

    def interleave(slices):
        gens = [one_pair(ds) for ds in slices]
        while gens:
            gens = [g for g in gens if next(g, "done") != "done"]

    if NP == UNR:
        interleave([pl.ds(k * PAIR, PAIR) for k in range(NP)])
    else:
        def body(i, carry):
            interleave([pl.ds(pl.multiple_of((i * UNR + k) * PAIR, PAIR), PAIR) for k in range(UNR)])
            return carry

        lax.fori_loop(0, NP // UNR, body, 0)


def _rwkv(u, B, L, D_R, rkv_blk, small_blk, params, spans, P, PL, UNR, riders):
    R = min(RWKV_BLOCK_ROWS, L)
    NC = L // R
    T = B * L
    NP = D_R // PAIR
    assert NP % UNR == 0 and R % RWKV_CHUNK == 0 and L % R == 0
    full = lambda a: pl.BlockSpec(a.shape, lambda b, c: (0,) * a.ndim)
    rowblk = lambda w, j: pl.BlockSpec((R, w), lambda b, c: (b * NC + c, j))
    rider_specs, rider_shapes = _rider_specs(riders, B * NC, lambda b, c: (b * NC + c, 0))
    body = functools.partial(_rwkv_kernel, NP=NP, UNR=UNR, P=P, PL=PL, spans=spans)
    res = pl.pallas_call(
        _with_cast_riders(body, 4 + len(params), 1, len(riders)),
        out_shape=[jax.ShapeDtypeStruct((T, D_R), BF16)] + rider_shapes,
        grid=(B, NC),
        in_specs=[rowblk(D_R, rkv_blk), rowblk(D_R, rkv_blk + 1), rowblk(D_R, rkv_blk + 2),
                  rowblk(SMALL_W, small_blk)] + [full(a) for a in params] + rider_specs,
        out_specs=[pl.BlockSpec((R, D_R), lambda b, c: (b * NC + c, 0))] + rider_specs,
        scratch_shapes=[
            pltpu.VMEM((HEAD, D_R), F32),
            pltpu.VMEM((SUBLANES, D_R), F32), pltpu.VMEM((SUBLANES, D_R), F32), pltpu.VMEM((SUBLANES, D_R), F32),
            pltpu.VMEM((SUBLANES, SMALL_W), F32),
            *[pltpu.VMEM((R, hi - lo), F32) for lo, hi in spans],
        ],
        compiler_params=pltpu.CompilerParams(
            dimension_semantics=("arbitrary", "arbitrary"), vmem_limit_bytes=VMEM_LIMIT),
        name="rwkv",
    )(u, u, u, u, *params, *riders)
    return res[0], res[1:]


LN_ROWS = 128


def _layer_norm_rows(o_ref, g_ref, b_ref, ob_ref=None):
    tm = o_ref.shape[0]

    def body(i, carry):
        rs = pl.ds(pl.multiple_of(i * LN_ROWS, LN_ROWS), LN_ROWS)
        h = o_ref[rs, :]
        mu = jnp.mean(h, axis=-1, keepdims=True)
        d = h - mu
        var = jnp.mean(d * d, axis=-1, keepdims=True)
        y = d * lax.rsqrt(var + LN_EPS) * g_ref[...] + b_ref[...]
        o_ref[rs, :] = y
        if ob_ref is not None:
            ob_ref[rs, :] = y.astype(BF16)
        return carry

    lax.fori_loop(0, tm // LN_ROWS, body, 0)


_LN_PARAMS = pltpu.CompilerParams(dimension_semantics=("arbitrary", "arbitrary"), vmem_limit_bytes=VMEM_LIMIT)


def _outproj_ln_kernel(y1_ref, y2_ref, w_ref, res_ref, g_ref, b_ref, o_ref, ob_ref, *, alpha, tn, nj):
    j = pl.program_id(1)

    @pl.when(j < nj)
    def _():
        k1 = y1_ref.shape[1]
        acc = jnp.dot(y1_ref[...], w_ref[0:k1, :], preferred_element_type=F32)
        acc = acc + jnp.dot(y2_ref[...], w_ref[k1:, :], preferred_element_type=F32)
        cs = pl.ds(pl.multiple_of(j * tn, tn), tn)
        o_ref[:, cs] = alpha * res_ref[...] + acc

    @pl.when(j == nj)
    def _():
        _layer_norm_rows(o_ref, g_ref, b_ref, ob_ref)


def _outproj_ln(y1, y2, w, res, g, b, alpha, tm, tn):
    T, K1 = y1.shape
    K2 = y2.shape[1]
    D = w.shape[1]
    assert w.shape[0] == K1 + K2
    nj = D // tn
    return pl.pallas_call(
        functools.partial(_outproj_ln_kernel, alpha=alpha, tn=tn, nj=nj),
        out_shape=(jax.ShapeDtypeStruct((T, D), F32), jax.ShapeDtypeStruct((T, D), BF16)),
        grid=(T // tm, nj + 1),
        in_specs=[
            pl.BlockSpec((tm, K1), lambda i, j: (i, 0)),
            pl.BlockSpec((tm, K2), lambda i, j: (i, 0)),
            pl.BlockSpec((K1 + K2, tn), lambda i, j: (0, jnp.minimum(j, nj - 1))),
            pl.BlockSpec((tm, tn), lambda i, j: (i, jnp.minimum(j, nj - 1))),
            pl.BlockSpec((1, D), lambda i, j: (0, 0)),
            pl.BlockSpec((1, D), lambda i, j: (0, 0)),
        ],
        out_specs=(pl.BlockSpec((tm, D), lambda i, j: (i, 0)), pl.BlockSpec((tm, D), lambda i, j: (i, 0))),
        compiler_params=_LN_PARAMS,
        name="outproj_ln",
    )(y1, y2, w, res, g, b)


def _ffn_ln_kernel(hb_ref, res_ref, wu_ref, wd_ref, g_ref, b_ref, o_ref, ob_ref, *, alpha, nf, tr):
    f = pl.program_id(1)

    @pl.when(f == 0)
    def _():
        o_ref[...] = jnp.zeros_like(o_ref)

    @pl.when(f < nf)
    def _():
        a = jnp.dot(hb_ref[...], wu_ref[...], preferred_element_type=F32)
        a = jnp.maximum(a, 0.0)
        a = (a * a).astype(BF16)
        o_ref[...] += jnp.dot(a, wd_ref[...], preferred_element_type=F32)

        cs = pl.ds(pl.multiple_of(f * tr, tr), tr)
        o_ref[:, cs] += alpha * res_ref[...]

    @pl.when(f == nf)
    def _():
        _layer_norm_rows(o_ref, g_ref, b_ref, ob_ref)


def _ffn_ln(hb, h, wu, wd, g, b, alpha, tm, tf):
    T, D = h.shape
    F = wu.shape[1]
    nf = F // tf
    tr = D // nf
    assert tr % LANES == 0
    return pl.pallas_call(
        functools.partial(_ffn_ln_kernel, alpha=alpha, nf=nf, tr=tr),
        out_shape=(jax.ShapeDtypeStruct((T, D), F32), jax.ShapeDtypeStruct((T, D), BF16)),
        grid=(T // tm, nf + 1),
        in_specs=[
            pl.BlockSpec((tm, D), lambda i, f: (i, 0)),
            pl.BlockSpec((tm, tr), lambda i, f: (i, jnp.minimum(f, nf - 1))),
            pl.BlockSpec((D, tf), lambda i, f: (0, jnp.minimum(f, nf - 1))),
            pl.BlockSpec((tf, D), lambda i, f: (jnp.minimum(f, nf - 1), 0)),
            pl.BlockSpec((1, D), lambda i, f: (0, 0)),
            pl.BlockSpec((1, D), lambda i, f: (0, 0)),
        ],
        out_specs=(pl.BlockSpec((tm, D), lambda i, f: (i, 0)), pl.BlockSpec((tm, D), lambda i, f: (i, 0))),
        compiler_params=_LN_PARAMS,
        name="ffn_ln",
    )(hb, h, wu, wd, g, b)


def _ple_ln_kernel(hb_ref, res_ref, p_ref, wg_ref, wp_ref, g_ref, b_ref, o_ref, *, alpha, tn, nj):
    j = pl.program_id(1)

    @pl.when(j < nj)
    def _():
        gate = jax.nn.sigmoid(jnp.dot(hb_ref[...], wg_ref[...], preferred_element_type=F32))
        emb = jnp.dot(p_ref[...].astype(BF16), wp_ref[...], preferred_element_type=F32)
        cs = pl.ds(pl.multiple_of(j * tn, tn), tn)
        o_ref[:, cs] = alpha * res_ref[...] + emb * gate

    @pl.when(j == nj)
    def _():
        _layer_norm_rows(o_ref, g_ref, b_ref)


def _ple_ln(hb, h, p2d, wg, wp, g, b, alpha, tm, tn):
    T, D = h.shape
    DP = p2d.shape[1]
    nj = D // tn
    return pl.pallas_call(
        functools.partial(_ple_ln_kernel, alpha=alpha, tn=tn, nj=nj),
        out_shape=jax.ShapeDtypeStruct((T, D), F32),
        grid=(T // tm, nj + 1),
        in_specs=[
            pl.BlockSpec((tm, D), lambda i, j: (i, 0)),
            pl.BlockSpec((tm, tn), lambda i, j: (i, jnp.minimum(j, nj - 1))),
            pl.BlockSpec((tm, DP), lambda i, j: (i, 0)),
            pl.BlockSpec((D, tn), lambda i, j: (0, jnp.minimum(j, nj - 1))),
            pl.BlockSpec((DP, tn), lambda i, j: (0, jnp.minimum(j, nj - 1))),
            pl.BlockSpec((1, D), lambda i, j: (0, 0)),
            pl.BlockSpec((1, D), lambda i, j: (0, 0)),
        ],
        out_specs=pl.BlockSpec((tm, D), lambda i, j: (i, 0)),
        compiler_params=_LN_PARAMS,
        name="ple_ln",
    )(hb, h, p2d, wg, wp, g, b)


def _pad_rows(w, rows, offset=0):
    out = jnp.zeros((rows, w.shape[1]), w.dtype)
    return out.at[offset:offset + w.shape[0]].set(w)


def _layer(h, p_i, w_in, conv_w, conv_b, dt_bias, A_log, D_skip, ssm_norm_g,
           rwkv_mu, w0, w_decay_b, a0, w_aaa_b, w_gate_b, k_k, k_a, r_k, gn_g, gn_b,
           w_out, ln1_g, ln1_b, w_up, w_down, ln2_g, ln2_b, w_ple, w_ple_gate, ln3_g, ln3_b, alpha):
    B, L, D = h.shape
    T = B * L
    D_SSM = ssm_norm_g.shape[0]
    H = dt_bias.shape[0]
    D_XBC = conv_w.shape[1]
    G = (D_XBC - D_SSM) // (2 * NSTATE)
    D_R = w0.shape[0]
    dl, al, gl = w_decay_b.shape[0], w_aaa_b.shape[0], w_gate_b.shape[0]
    dt_off = dl + al + gl
    assert D_XBC == 2 * D_SSM and D_R == D_SSM and D_SSM == H * HEAD and H == 4 * G
    assert dt_off + H <= SMALL_W and dt_off // LANES == (dt_off + H - 1) // LANES
    assert L % SSD_CHUNK == 0 and L % RWKV_CHUNK == 0 and D_R % PAIR == 0

    o_dt = D_SSM + D_XBC
    o_r = o_dt + H
    pad = SMALL_W - (dt_off + H)
    wt_cat = _reorder_rows(w_in.T, [(0, o_dt), (o_r, 3 * D_R + dt_off), (o_dt, H), (None, pad)])
    small_blk = (D_SSM + D_XBC + 3 * D_R) // SMALL_W
    rkv_blk = (D_SSM + D_XBC) // D_R

    tm = min(ROW_TILE, T)
    tn_in = next(t for t in INPROJ_COL_TILES if wt_cat.shape[0] % t == 0)
    tn = min(LN_COL_TILE, D)
    u = _inproj(h.reshape(T, D), wt_cat, tm, tn_in)

    row = lambda a: a.reshape(1, -1).astype(F32)
    lane_tile = lambda a: jnp.zeros((1, LANES), F32).at[0, dt_off % LANES:dt_off % LANES + H].set(a)
    hl = jnp.arange(LANES)[:, None] - dt_off % LANES
    emat = (hl == (jnp.arange(D_SSM)[None, :] // HEAD)).astype(BF16)
    y_ssm, (wo, wpg) = _ssd(
        u, B, L, D_SSM, G, H, dt_off, small_blk,
        conv_w[:, :D_SSM], row(conv_b[:D_SSM]), conv_w[:, D_SSM:], row(conv_b[D_SSM:]),
        lane_tile(dt_bias), lane_tile(A_log), row(jnp.repeat(D_skip, HEAD)), row(ssm_norm_g), emat,
        P=SSD_PIECES, riders=[w_out, w_ple_gate])

    mu = rwkv_mu
    mus = jnp.concatenate([mu[3 * D_R:], jnp.zeros((SMALL_W - dt_off,), F32)])
    spans, lora_w = [], []
    for lo, w in ((0, w_decay_b), (dl, w_aaa_b), (dl + al, w_gate_b)):
        lo_al = lo // LANES * LANES
        hi_al = -(-(lo + w.shape[0]) // LANES) * LANES
        spans.append((lo_al, hi_al))
        lora_w.append(_pad_rows(w, hi_al - lo_al, lo - lo_al))
    params = [row(mu[:D_R]), row(mu[D_R:2 * D_R]), row(mu[2 * D_R:3 * D_R]), row(mus),
              row(w0), row(a0), row(k_k), row(k_a), row(r_k), row(gn_g), row(gn_b), *lora_w]
    y_rwkv, (wu, wdn) = _rwkv(u, B, L, D_R, rkv_blk, small_blk, params, tuple(spans), P=RWKV_PIECES,
                              PL=LORA_PIECES, UNR=min(RWKV_UNROLL, D_R // PAIR), riders=[w_up, w_down])

    h1, h1b = _outproj_ln(y_ssm, y_rwkv, wo, h.reshape(T, D), row(ln1_g), row(ln1_b), alpha, tm, tn)
    h2, h2b = _ffn_ln(h1b, h1, wu, wdn, row(ln2_g), row(ln2_b), alpha, tm, min(FFN_TILE, wu.shape[1]))
    h3 = _ple_ln(h2b, h2, p_i.reshape(T, -1), wpg, w_ple.astype(BF16), row(ln3_g), row(ln3_b), alpha, tm, tn)
    return h3.reshape(B, L, D)


def kernel(x, p, w_in, conv_w, conv_b, dt_bias, A_log, D_skip, ssm_norm_g, rwkv_mu, w0, w_decay_b, a0, w_aaa_b, w_gate_b, k_k, k_a, r_k, gn_g, gn_b, w_out, ln1_g, ln1_b, w_up, w_down, ln2_g, ln2_b, w_ple, w_ple_gate, ln3_g, ln3_b):
    depth = w_in.shape[0]
    alpha = float((2 * depth) ** 0.25)
    h = x
    for i in range(depth):
        h = _layer(h, p[i], w_in[i], conv_w[i], conv_b[i], dt_bias[i], A_log[i], D_skip[i], ssm_norm_g[i],
                   rwkv_mu[i], w0[i], w_decay_b[i], a0[i], w_aaa_b[i], w_gate_b[i], k_k[i], k_a[i],
                   r_k[i].reshape(-1), gn_g[i], gn_b[i], w_out[i], ln1_g[i], ln1_b[i], w_up[i], w_down[i],
                   ln2_g[i], ln2_b[i], w_ple[i], w_ple_gate[i], ln3_g[i], ln3_b[i], alpha)
    return h
```
